```python
import jax
import jax.numpy as jnp
from jax import lax
import numpy as np

D_MODEL = 1024
BATCH = 8
SEQ = 4096
DEPTH = 2

MEM_LEN = 256
GRID_W = 64
EPS = 1e-6
NEG_INF = -1e30
HEAD_DIM = 64
D_FF = 11 * D_MODEL // 4
CONV_CH = D_MODEL // 4
CONV_WIDTH = 31
WIN_HEADS = (D_MODEL // 2) // HEAD_DIM
WIN_KV_HEADS = 2
WIN_GROUP = WIN_HEADS // WIN_KV_HEADS
WINDOW = 128
BLOCK = 128
T5_BUCKETS = 32
T5_MAX_DIST = 128
NA_HEADS = (D_MODEL // 4) // HEAD_DIM
NA_ROWS_MAX = 8
NA_COLS = 16
X_HEADS = 4
X_HEAD_DIM = D_MODEL // X_HEADS
N_BRANCH = 3
IN_WIDTHS = (2 * CONV_CH,
             WIN_HEADS * HEAD_DIM, WIN_KV_HEADS * HEAD_DIM, WIN_KV_HEADS * HEAD_DIM,
             NA_HEADS * HEAD_DIM, NA_HEADS * HEAD_DIM, NA_HEADS * HEAD_DIM,
             N_BRANCH * D_MODEL)
IN_WIDTH = sum(IN_WIDTHS)
IN_SPLITS = tuple(sum(IN_WIDTHS[:i + 1]) for i in range(len(IN_WIDTHS) - 1))

kernel_name = 'hybrid_conv_window_natten_encoder'


def rms_norm(x, g):
    xf = x.astype(jnp.float32)
    y = xf * lax.rsqrt(jnp.mean(xf * xf, axis=-1, keepdims=True) + EPS)
    return (y * g.astype(jnp.float32)).astype(x.dtype)


def layer_norm(x, g, b):
    xf = x.astype(jnp.float32)
    mu = jnp.mean(xf, axis=-1, keepdims=True)
    var = jnp.mean(jnp.square(xf - mu), axis=-1, keepdims=True)
    y = (xf - mu) * lax.rsqrt(var + EPS)
    return (y * g.astype(jnp.float32) + b.astype(jnp.float32)).astype(x.dtype)


def swiglu(x, w_gate, w_up, w_down):
    return (jax.nn.silu(x @ w_gate) * (x @ w_up)) @ w_down


def conformer_conv(u, dw_w, dw_b, ln_g, ln_b):
    a, gate = jnp.split(u, 2, axis=-1)
    z = a * jax.nn.sigmoid(gate)
    z = lax.conv_general_dilated(
        z, dw_w[:, None, :].astype(z.dtype), window_strides=(1,),
        padding=((CONV_WIDTH // 2, CONV_WIDTH // 2),),
        dimension_numbers=('NWC', 'WIO', 'NWC'),
        feature_group_count=CONV_CH) + dw_b
    z = layer_norm(z, ln_g, ln_b)
    return jax.nn.silu(z)


def t5_buckets(rel):
    half = T5_BUCKETS // 2
    max_exact = half // 2
    ret = (rel > 0).astype(np.int32) * half
    n = np.abs(rel)
    large = max_exact + (np.log(np.maximum(n, 1) / max_exact)
                         / np.log(T5_MAX_DIST / max_exact) * (half - max_exact)).astype(np.int32)
    large = np.minimum(large, half - 1)
    return ret + np.where(n < max_exact, n, large)


def windowed_gqa(q, k, v, sink, t5_table):
    bsz, seq = q.shape[0], q.shape[1]
    nblk = seq // BLOCK
    span = BLOCK + 2 * WINDOW
    rel = np.arange(span)[None, :] - WINDOW - np.arange(BLOCK)[:, None]
    band = np.abs(rel) <= WINDOW
    bias = jnp.transpose(t5_table[t5_buckets(rel)], (2, 0, 1)).astype(jnp.float32)
    bias = bias.reshape(WIN_KV_HEADS, WIN_GROUP, BLOCK, span)
    sink_g = sink.astype(jnp.float32).reshape(WIN_KV_HEADS, WIN_GROUP)[:, :, None, None]
    kp = jnp.pad(k, ((0, 0), (WINDOW, WINDOW), (0, 0), (0, 0)))
    vp = jnp.pad(v, ((0, 0), (WINDOW, WINDOW), (0, 0), (0, 0)))
    scale = HEAD_DIM ** -0.5

    def one_block(i):
        start = i * BLOCK
        qb = lax.dynamic_slice_in_dim(q, start, BLOCK, axis=1)
        kb = lax.dynamic_slice_in_dim(kp, start, span, axis=1)
        vb = lax.dynamic_slice_in_dim(vp, start, span, axis=1)
        s = jnp.einsum('bqkgd,bskd->bkgqs', qb, kb).astype(jnp.float32) * scale + bias
        kpos = start - WINDOW + jnp.arange(span)
        valid = jnp.logical_and(band, ((kpos >= 0) & (kpos < seq))[None, :])
        s = jnp.where(valid, s, NEG_INF)
        m = jnp.maximum(jnp.max(s, axis=-1, keepdims=True), sink_g)
        p = jnp.exp(s - m)
        p = p / (jnp.sum(p, axis=-1, keepdims=True) + jnp.exp(sink_g - m))
        o = jnp.einsum('bkgqs,bskd->bqkgd', p.astype(vb.dtype), vb)
        return o.reshape(bsz, BLOCK, WIN_HEADS * HEAD_DIM)

    out = lax.map(one_block, jnp.arange(nblk))
    return jnp.transpose(out, (1, 0, 2, 3)).reshape(bsz, seq, WIN_HEADS * HEAD_DIM)


def neighbourhood_attn(q, k, v, rpb):
    bsz, seq = q.shape[0], q.shape[1]
    rows = seq // GRID_W
    wr = min(NA_ROWS_MAX, rows)
    qg = q.reshape(bsz, rows, GRID_W, NA_HEADS, HEAD_DIM)
    kg = k.reshape(bsz, rows, GRID_W, NA_HEADS, HEAD_DIM)
    vg = v.reshape(bsz, rows, GRID_W, NA_HEADS, HEAD_DIM)
    col = np.arange(GRID_W)
    col_start = np.clip(col - NA_COLS // 2, 0, GRID_W - NA_COLS)
    col_idx = col_start[:, None] + np.arange(NA_COLS)[None, :]
    dcol = col_idx - col[:, None]
    rpb_c = rpb[:, :, dcol + NA_COLS - 1].astype(jnp.float32)
    scale = HEAD_DIM ** -0.5

    def one_row(r):
        rs = jnp.clip(r - wr // 2, 0, rows - wr)
        qr = lax.dynamic_index_in_dim(qg, r, axis=1, keepdims=False)
        kr = lax.dynamic_slice_in_dim(kg, rs, wr, axis=1)
        vr = lax.dynamic_slice_in_dim(vg, rs, wr, axis=1)
        kn = kr[:, :, col_idx]
        vn = vr[:, :, col_idx]
        drow = rs + jnp.arange(wr) - r
        bias = jnp.transpose(rpb_c[:, drow + NA_ROWS_MAX - 1], (0, 2, 1, 3))
        s = jnp.einsum('bchd,bwcjhd->bhcwj', qr, kn).astype(jnp.float32) * scale + bias
        p = jax.nn.softmax(s.reshape(bsz, NA_HEADS, GRID_W, wr * NA_COLS), axis=-1)
        p = p.reshape(bsz, NA_HEADS, GRID_W, wr, NA_COLS)
        o = jnp.einsum('bhcwj,bwcjhd->bchd', p.astype(vn.dtype), vn)
        return o.reshape(bsz, GRID_W, NA_HEADS * HEAD_DIM)

    out = lax.map(one_row, jnp.arange(rows))
    return jnp.transpose(out, (1, 0, 2, 3)).reshape(bsz, seq, NA_HEADS * HEAD_DIM)


def memory_cross_attn(h, mem_n, w_q, w_kv, w_o):
    bsz, seq = h.shape[0], h.shape[1]
    q = (h @ w_q).reshape(bsz, seq, X_HEADS, X_HEAD_DIM)
    k, v = jnp.split(mem_n @ w_kv, 2, axis=-1)
    k = k.reshape(bsz, -1, X_HEADS, X_HEAD_DIM)
    v = v.reshape(bsz, -1, X_HEADS, X_HEAD_DIM)
    s = jnp.einsum('bqhd,bmhd->bhqm', q, k).astype(jnp.float32) * (X_HEAD_DIM ** -0.5)
    p = jax.nn.softmax(s, axis=-1)
    o = jnp.einsum('bhqm,bmhd->bqhd', p.astype(v.dtype), v)
    return o.reshape(bsz, seq, X_HEADS * X_HEAD_DIM) @ w_o


def setup_inputs(seed: int = 0) -> dict:
    key = jax.random.key(seed)
    keys = iter(jax.random.split(key, 64))
    f32 = jnp.float32

    def w(shape, fan_in):
        return jax.random.normal(next(keys), shape, f32) * fan_in ** -0.5

    def gain(shape):
        return 1.0 + 0.02 * jax.random.normal(next(keys), shape, f32)

    def small(shape, s):
        return s * jax.random.normal(next(keys), shape, f32)

    L, D = DEPTH, D_MODEL
    return {
        'x': jax.random.normal(next(keys), (BATCH, SEQ, D), f32),
        'mem': jax.random.normal(next(keys), (BATCH, MEM_LEN, D), f32),
        'norm_ffn1': gain((L, D)),
        'ffn1_w_gate': w((L, D, D_FF), D),
        'ffn1_w_up': w((L, D, D_FF), D),
        'ffn1_w_down': w((L, D_FF, D), D_FF),
        'norm_mix': gain((L, D)),
        'w_in': w((L, D, IN_WIDTH), D),
        'conv_dw_w': w((L, CONV_WIDTH, CONV_CH), CONV_WIDTH),
        'conv_dw_b': small((L, CONV_CH), 0.02),
        'conv_ln_g': gain((L, CONV_CH)),
        'conv_ln_b': small((L, CONV_CH), 0.02),
        'conv_w_out': w((L, CONV_CH, D), CONV_CH),
        'win_sink': small((L, WIN_HEADS), 0.5),
        't5_bias': small((T5_BUCKETS, WIN_HEADS), 0.1),
        'win_w_out': w((L, WIN_HEADS * HEAD_DIM, D), WIN_HEADS * HEAD_DIM),
        'na_rpb': small((L, NA_HEADS, 2 * NA_ROWS_MAX - 1, 2 * NA_COLS - 1), 0.1),
        'na_w_out': w((L, NA_HEADS * HEAD_DIM, D), NA_HEADS * HEAD_DIM),
        'w_out': w((L, D, D), D),
        'norm_cross': gain((L, D)),
        'norm_mem': gain((L, D)),
        'cross_w_q': w((L, D, X_HEADS * X_HEAD_DIM), D),
        'cross_w_kv': w((L, D, 2 * X_HEADS * X_HEAD_DIM), D),
        'cross_w_o': w((L, X_HEADS * X_HEAD_DIM, D), X_HEADS * X_HEAD_DIM),
        'norm_ffn2': gain((L, D)),
        'ffn2_w_gate': w((L, D, D_FF), D),
        'ffn2_w_up': w((L, D, D_FF), D),
        'ffn2_w_down': w((L, D_FF, D), D_FF),
        'norm_final': gain((D,)),
    }


def reference(x, mem, norm_ffn1, ffn1_w_gate, ffn1_w_up, ffn1_w_down, norm_mix, w_in,
              conv_dw_w, conv_dw_b, conv_ln_g, conv_ln_b, conv_w_out, win_sink, t5_bias,
              win_w_out, na_rpb, na_w_out, w_out, norm_cross, norm_mem, cross_w_q,
              cross_w_kv, cross_w_o, norm_ffn2, ffn2_w_gate, ffn2_w_up, ffn2_w_down,
              norm_final):
    bsz, seq, _ = x.shape
    for l in range(DEPTH):
        x = x + 0.5 * swiglu(rms_norm(x, norm_ffn1[l]), ffn1_w_gate[l], ffn1_w_up[l], ffn1_w_down[l])
        h = rms_norm(x, norm_mix[l])
        u_a, bq, bk, bv, cq, ck, cv, gates = jnp.split(h @ w_in[l], IN_SPLITS, axis=-1)
        y_a = conformer_conv(u_a, conv_dw_w[l], conv_dw_b[l], conv_ln_g[l], conv_ln_b[l]) @ conv_w_out[l]
        y_b = windowed_gqa(
            bq.reshape(bsz, seq, WIN_KV_HEADS, WIN_GROUP, HEAD_DIM),
            bk.reshape(bsz, seq, WIN_KV_HEADS, HEAD_DIM),
            bv.reshape(bsz, seq, WIN_KV_HEADS, HEAD_DIM),
            win_sink[l], t5_bias) @ win_w_out[l]
        y_c = neighbourhood_attn(
            cq.reshape(bsz, seq, NA_HEADS, HEAD_DIM),
            ck.reshape(bsz, seq, NA_HEADS, HEAD_DIM),
            cv.reshape(bsz, seq, NA_HEADS, HEAD_DIM),
            na_rpb[l]) @ na_w_out[l]
        g = jax.nn.sigmoid(gates).reshape(bsz, seq, N_BRANCH, D_MODEL)
        y = g[:, :, 0] * y_a + g[:, :, 1] * y_b + g[:, :, 2] * y_c
        x = x + y @ w_out[l]
        x = x + memory_cross_attn(rms_norm(x, norm_cross[l]), rms_norm(mem, norm_mem[l]),
                                  cross_w_q[l], cross_w_kv[l], cross_w_o[l])
        x = x + 0.5 * swiglu(rms_norm(x, norm_ffn2[l]), ffn2_w_gate[l], ffn2_w_up[l], ffn2_w_down[l])
    return rms_norm(x, norm_final)
```

```python
import functools

import numpy as np
import jax
import jax.numpy as jnp
from jax import lax
from jax.experimental import pallas as pl
from jax.experimental.pallas import tpu as pltpu

F32 = jnp.float32
BF16 = jnp.bfloat16

D_MODEL = 1024
EPS = 1e-6
NEG_INF = -1e30
HEAD_DIM = 64
D_FF = 11 * D_MODEL // 4
CONV_CH = D_MODEL // 4
CONV_WIDTH = 31
WIN_HEADS = (D_MODEL // 2) // HEAD_DIM
WIN_KV_HEADS = 2
WIN_GROUP = WIN_HEADS // WIN_KV_HEADS
WINDOW = 128
BLOCK = 128
T5_BUCKETS = 32
T5_MAX_DIST = 128
NA_HEADS = (D_MODEL // 4) // HEAD_DIM
NA_ROWS = 8
NA_COLS = 16
GRID_W = 64
X_HEADS = 4
X_HEAD_DIM = D_MODEL // X_HEADS
N_BRANCH = 3

WIN_Q = WIN_HEADS * HEAD_DIM
WIN_KV = WIN_KV_HEADS * HEAD_DIM
NA_W = NA_HEADS * HEAD_DIM
OFF_CONV = 0
OFF_BQ = OFF_CONV + 2 * CONV_CH
OFF_BK = OFF_BQ + WIN_Q
OFF_BV = OFF_BK + WIN_KV
OFF_CQ = OFF_BV + WIN_KV
OFF_CK = OFF_CQ + NA_W
OFF_CV = OFF_CK + NA_W
OFF_GATE = OFF_CV + NA_W
IN_WIDTH = OFF_GATE + N_BRANCH * D_MODEL

VMEM_LIMIT_BYTES = 56 * 1024 * 1024
TOKEN_TILE = 512
FF_CHUNK = 256
CONV_ROWS = 128
CONV_PAD = 16
NA_Q_ROWS = 8


def _params(*sem):
    return pltpu.CompilerParams(dimension_semantics=sem, vmem_limit_bytes=VMEM_LIMIT_BYTES)


def _resident(shape):
    zeros = (0,) * len(shape)
    return pl.BlockSpec(shape, lambda *_: zeros, pipeline_mode=pl.Buffered(1))


def _rms(x, g):
    ms = jnp.mean(x * x, axis=-1, keepdims=True)
    return x * lax.rsqrt(ms + EPS) * g


def _dot(a, b):
    return jnp.dot(a, b, preferred_element_type=F32)


def _dot_nt(a, b):
    return lax.dot_general(a, b, (((1,), (1,)), ((), ())), preferred_element_type=F32)


def _ffn_kernel(x_ref, g_ref, wg_ref, wu_ref, wd_ref, gf_ref, o_ref, act_ref, *, final_norm):
    x = x_ref[...]
    h = _rms(x, g_ref[...]).astype(BF16)
    for c in range(D_FF // FF_CHUNK):
        sl = slice(c * FF_CHUNK, (c + 1) * FF_CHUNK)
        a = _dot(h, wg_ref[:, sl])
        b = _dot(h, wu_ref[:, sl])
        act_ref[:, sl] = (a * jax.nn.sigmoid(a) * b).astype(BF16)
    y = x + 0.5 * _dot(act_ref[...], wd_ref[...])
    if final_norm:
        y = _rms(y, gf_ref[...])
    o_ref[...] = y


def _ffn(x, g, wg, wu, wd, gf, final_norm):
    t, d = x.shape
    tile = pl.BlockSpec((TOKEN_TILE, d), lambda i: (i, 0))
    return pl.pallas_call(
        functools.partial(_ffn_kernel, final_norm=final_norm),
        grid=(t // TOKEN_TILE,),
        in_specs=[tile, _resident((1, d)), _resident((d, D_FF)), _resident((d, D_FF)),
                  _resident((D_FF, d)), _resident((1, d))],
        out_specs=tile,
        out_shape=jax.ShapeDtypeStruct((t, d), F32),
        scratch_shapes=[pltpu.VMEM((TOKEN_TILE, D_FF), BF16)],
        compiler_params=_params("parallel"),
        name="ffn",
    )(x, g, wg, wu, wd, gf)


def _inproj_kernel(x_ref, g_ref, w_ref, z_ref, bq_ref, bk_ref, bv_ref, cq_ref, ck_ref, cv_ref, gate_ref):
    h = _rms(x_ref[...], g_ref[...]).astype(BF16)
    scale = HEAD_DIM ** -0.5
    ua = _dot(h, w_ref[:, OFF_CONV:OFF_BQ])
    z_ref[...] = ua[:, :CONV_CH] * jax.nn.sigmoid(ua[:, CONV_CH:])
    bq_ref[...] = (_dot(h, w_ref[:, OFF_BQ:OFF_BK]) * scale).astype(BF16)
    bkv = _dot(h, w_ref[:, OFF_BK:OFF_CQ])
    bk_ref[...] = bkv[:, :WIN_KV].astype(BF16)
    bv_ref[...] = bkv[:, WIN_KV:].astype(BF16)
    cq_ref[...] = (_dot(h, w_ref[:, OFF_CQ:OFF_CK]) * scale).astype(BF16)
    ck_ref[...] = _dot(h, w_ref[:, OFF_CK:OFF_CV]).astype(BF16)
    cv_ref[...] = _dot(h, w_ref[:, OFF_CV:OFF_GATE]).astype(BF16)
    for c in range(N_BRANCH):
        lo = OFF_GATE + c * D_MODEL
        gate_ref[:, c * D_MODEL:(c + 1) * D_MODEL] = _dot(h, w_ref[:, lo:lo + D_MODEL]).astype(BF16)


def _inproj(x, g, w):
    t, d = x.shape
    widths = (CONV_CH, WIN_Q, WIN_KV, WIN_KV, NA_W, NA_W, NA_W, N_BRANCH * D_MODEL)
    dtypes = (F32,) + (BF16,) * 7

    def tile(n):
        return pl.BlockSpec((TOKEN_TILE, n), lambda i: (i, 0))

    return pl.pallas_call(
        _inproj_kernel,
        grid=(t // TOKEN_TILE,),
        in_specs=[tile(d), _resident((1, d)), _resident((d, IN_WIDTH))],
        out_specs=[tile(n) for n in widths],
        out_shape=[jax.ShapeDtypeStruct((t, n), dt) for n, dt in zip(widths, dtypes)],
        compiler_params=_params("parallel"),
        name="inproj",
    )(x, g, w)


def _conv_kernel(z_ref, w_ref, b_ref, lg_ref, lb_ref, o_ref, zp_ref, *, seq):
    zeros = jnp.zeros((CONV_PAD, CONV_CH), F32)
    zp_ref[0:CONV_PAD, :] = zeros
    zp_ref[CONV_PAD + seq:CONV_PAD + seq + CONV_PAD, :] = zeros
    zp_ref[CONV_PAD:CONV_PAD + seq, :] = z_ref[0]
    w = w_ref[...]
    half = CONV_WIDTH // 2
    lead = CONV_PAD - half
    n_shift = 8
    span = CONV_ROWS + 2 * CONV_PAD - n_shift

    def chunk(c, carry):
        base = pl.multiple_of(c * CONV_ROWS, CONV_ROWS)
        slab = zp_ref[pl.ds(base, CONV_ROWS + 2 * CONV_PAD), :]
        acc = jnp.zeros((CONV_ROWS, CONV_CH), F32) + b_ref[...]
        for s in range(n_shift):
            shifted = slab[s:s + span, :]
            for a in range(0, span - CONV_ROWS + 1, n_shift):
                k = a + s - lead
                if 0 <= k < CONV_WIDTH:
                    acc = acc + shifted[a:a + CONV_ROWS, :] * w[k:k + 1, :]
        mu = jnp.mean(acc, axis=-1, keepdims=True)
        cen = acc - mu
        var = jnp.mean(cen * cen, axis=-1, keepdims=True)
        y = cen * lax.rsqrt(var + EPS) * lg_ref[...] + lb_ref[...]
        o_ref[0, pl.ds(base, CONV_ROWS), :] = (y * jax.nn.sigmoid(y)).astype(BF16)
        return carry

    lax.fori_loop(0, seq // CONV_ROWS, chunk, 0)


def _conv(z, w, b, lg, lb):
    bsz, seq, ch = z.shape
    blk = pl.BlockSpec((1, seq, ch), lambda i: (i, 0, 0))
    return pl.pallas_call(
        functools.partial(_conv_kernel, seq=seq),
        grid=(bsz,),
        in_specs=[blk, _resident((CONV_WIDTH, ch)), _resident((1, ch)), _resident((1, ch)),
                  _resident((1, ch))],
        out_specs=blk,
        out_shape=jax.ShapeDtypeStruct((bsz, seq, ch), BF16),
        scratch_shapes=[pltpu.VMEM((seq + 2 * CONV_PAD, ch), F32)],
        compiler_params=_params("parallel"),
        name="conv",
    )(z, w, b, lg, lb)


def _t5_buckets(rel):
    half = T5_BUCKETS // 2
    max_exact = half // 2
    ret = (rel > 0).astype(np.int32) * half
    n = np.abs(rel)
    large = max_exact + (np.log(np.maximum(n, 1) / max_exact)
                         / np.log(T5_MAX_DIST / max_exact) * (half - max_exact)).astype(np.int32)
    large = np.minimum(large, half - 1)
    return ret + np.where(n < max_exact, n, large)


def _win_bias(t5_table):
    span = BLOCK + 2 * WINDOW
    rel = np.arange(span)[None, :] - WINDOW - np.arange(BLOCK)[:, None]
    band = np.abs(rel) <= WINDOW
    bias = jnp.transpose(t5_table[_t5_buckets(rel)], (2, 0, 1)).astype(F32)
    return jnp.where(band[None], bias, NEG_INF)


def _win_kernel(sink_ref, q_ref, kp_ref, kc_ref, kn_ref, vp_ref, vc_ref, vn_ref, bias_ref, o_ref):
    i = pl.program_id(1)
    last = pl.num_programs(1) - 1
    span = BLOCK + 2 * WINDOW
    col = lax.broadcasted_iota(jnp.int32, (1, span), 1)
    outside = ((col < WINDOW) & (i == 0)) | ((col >= WINDOW + BLOCK) & (i == last))
    pen = jnp.where(outside, NEG_INF, 0.0).astype(F32)
    k = jnp.concatenate([kp_ref[0], kc_ref[0], kn_ref[0]], axis=0)
    v = jnp.concatenate([vp_ref[0], vc_ref[0], vn_ref[0]], axis=0)
    q = q_ref[0]
    outs = []
    for h in range(WIN_HEADS):
        kv = h // WIN_GROUP
        hs = slice(h * HEAD_DIM, (h + 1) * HEAD_DIM)
        kvs = slice(kv * HEAD_DIM, (kv + 1) * HEAD_DIM)
        s = _dot_nt(q[:, hs], k[:, kvs]) + bias_ref[h] + pen
        sink = sink_ref[h]
        m = jnp.maximum(jnp.max(s, axis=-1, keepdims=True), sink)
        p = jnp.exp(s - m)
        den = jnp.sum(p, axis=-1, keepdims=True) + jnp.exp(sink - m)
        outs.append(_dot(p.astype(BF16), v[:, kvs]) / den)
    o_ref[0] = jnp.concatenate(outs, axis=-1).astype(BF16)


def _win_attn(q, k, v, sink, bias):
    bsz, seq, _ = q.shape
    nblk = seq // BLOCK

    def kv_spec(shift):
        return pl.BlockSpec((1, BLOCK, WIN_KV),
                            lambda b, i: (b, jnp.clip(i + shift, 0, nblk - 1), 0))

    qspec = pl.BlockSpec((1, BLOCK, WIN_Q), lambda b, i: (b, i, 0))
    return pl.pallas_call(
        _win_kernel,
        grid=(bsz, nblk),
        in_specs=[pl.BlockSpec(memory_space=pltpu.SMEM), qspec,
                  kv_spec(-1), kv_spec(0), kv_spec(1), kv_spec(-1), kv_spec(0), kv_spec(1),
                  _resident(bias.shape)],
        out_specs=qspec,
        out_shape=jax.ShapeDtypeStruct((bsz, seq, WIN_Q), BF16),
        compiler_params=_params("parallel", "parallel"),
        name="win_attn",
    )(sink, q, k, k, k, v, v, v, bias)


def _na_bias(rpb):
    c = np.arange(GRID_W)
    col_start = np.clip(c - NA_COLS // 2, 0, GRID_W - NA_COLS)
    col = np.arange(GRID_W)
    inside = (col[None, :] >= col_start[:, None]) & (col[None, :] < col_start[:, None] + NA_COLS)
    dcol = np.clip(col[None, :] - c[:, None] + NA_COLS - 1, 0, 2 * NA_COLS - 2)
    drow = np.arange(NA_ROWS)[None, :] - np.arange(NA_ROWS)[:, None] + NA_ROWS - 1
    b = rpb.astype(F32)[:, drow]
    b = b[:, :, :, dcol]
    b = jnp.where(inside[None, None, None], b, NEG_INF)
    b = jnp.transpose(b, (1, 0, 3, 2, 4))
    return b.reshape(NA_ROWS, NA_HEADS, GRID_W, NA_ROWS * GRID_W)


def _na_kernel(q_ref, k_ref, v_ref, bias_ref, o_ref, *, rows):
    j = pl.program_id(1)
    win = NA_ROWS * GRID_W
    for t in range(NA_Q_ROWS):
        r = j * NA_Q_ROWS + t
        rs = jnp.clip(r - NA_ROWS // 2, 0, rows - NA_ROWS)
        start = pl.multiple_of(rs * GRID_W, GRID_W)
        kw = k_ref[0, pl.ds(start, win), :]
        vw = v_ref[0, pl.ds(start, win), :]
        qr = q_ref[0, t * GRID_W:(t + 1) * GRID_W, :]
        outs = []
        for h in range(NA_HEADS):
            hs = slice(h * HEAD_DIM, (h + 1) * HEAD_DIM)
            s = _dot_nt(qr[:, hs], kw[:, hs]) + bias_ref[r - rs, h]
            m = jnp.max(s, axis=-1, keepdims=True)
            p = jnp.exp(s - m)
            den = jnp.sum(p, axis=-1, keepdims=True)
            outs.append(_dot(p.astype(BF16), vw[:, hs]) / den)
        o_ref[0, t * GRID_W:(t + 1) * GRID_W, :] = jnp.concatenate(outs, axis=-1).astype(BF16)


def _na_attn(q, k, v, bias):
    bsz, seq, _ = q.shape
    rows = seq // GRID_W
    qspec = pl.BlockSpec((1, NA_Q_ROWS * GRID_W, NA_W), lambda b, j: (b, j, 0))
    kvspec = pl.BlockSpec((1, seq, NA_W), lambda b, j: (b, 0, 0))
    return pl.pallas_call(
        functools.partial(_na_kernel, rows=rows),
        grid=(bsz, rows // NA_Q_ROWS),
        in_specs=[qspec, kvspec, kvspec, _resident(bias.shape)],
        out_specs=qspec,
        out_shape=jax.ShapeDtypeStruct((bsz, seq, NA_W), BF16),
        compiler_params=_params("parallel", "arbitrary"),
        name="na_attn",
    )(q, k, v, bias)


def _mixout_kernel(x_ref, za_ref, ob_ref, oc_ref, gate_ref, wa_ref, wb_ref, wc_ref, wo_ref, o_ref):
    def gate(c):
        return jax.nn.sigmoid(gate_ref[:, c * D_MODEL:(c + 1) * D_MODEL].astype(F32))

    y = gate(0) * _dot(za_ref[...], wa_ref[...])
    y = y + gate(1) * _dot(ob_ref[...], wb_ref[...])
    y = y + gate(2) * _dot(oc_ref[...], wc_ref[...])
    o_ref[...] = x_ref[...] + _dot(y.astype(BF16), wo_ref[...])


def _mixout(x, za, ob, oc, gates, wa, wb, wc, wo):
    t, d = x.shape

    def tile(n):
        return pl.BlockSpec((TOKEN_TILE, n), lambda i: (i, 0))

    return pl.pallas_call(
        _mixout_kernel,
        grid=(t // TOKEN_TILE,),
        in_specs=[tile(d), tile(CONV_CH), tile(WIN_Q), tile(NA_W), tile(N_BRANCH * d),
                  _resident(wa.shape), _resident(wb.shape), _resident(wc.shape), _resident(wo.shape)],
        out_specs=tile(d),
        out_shape=jax.ShapeDtypeStruct((t, d), F32),
        compiler_params=_params("parallel"),
        name="mixout",
    )(x, za, ob, oc, gates, wa, wb, wc, wo)


def _memkv_kernel(m_ref, g_ref, w_ref, k_ref, v_ref):
    h = _rms(m_ref[0], g_ref[...]).astype(BF16)
    k_ref[0] = _dot(h, w_ref[:, :D_MODEL]).astype(BF16)
    v_ref[0] = _dot(h, w_ref[:, D_MODEL:]).astype(BF16)


def _memkv(mem, g, w):
    bsz, mlen, d = mem.shape
    blk = pl.BlockSpec((1, mlen, d), lambda b: (b, 0, 0))
    return pl.pallas_call(
        _memkv_kernel,
        grid=(bsz,),
        in_specs=[blk, _resident((1, d)), _resident(w.shape)],
        out_specs=[blk, blk],
        out_shape=[jax.ShapeDtypeStruct((bsz, mlen, d), BF16)] * 2,
        compiler_params=_params("parallel"),
        name="memkv",
    )(mem, g, w)


def _cross_kernel(x_ref, g_ref, k_ref, v_ref, wq_ref, wo_ref, o_ref):
    x = x_ref[0]
    h = _rms(x, g_ref[...]).astype(BF16)
    scale = X_HEAD_DIM ** -0.5
    q = (_dot(h, wq_ref[...]) * scale).astype(BF16)
    k = k_ref[0]
    v = v_ref[0]
    outs = []
    for hd in range(X_HEADS):
        hs = slice(hd * X_HEAD_DIM, (hd + 1) * X_HEAD_DIM)
        s = _dot_nt(q[:, hs], k[:, hs])
        m = jnp.max(s, axis=-1, keepdims=True)
        p = jnp.exp(s - m)
        den = jnp.sum(p, axis=-1, keepdims=True)
        outs.append((_dot(p.astype(BF16), v[:, hs]) / den).astype(BF16))
    o = jnp.concatenate(outs, axis=-1)
    o_ref[0] = x + _dot(o, wo_ref[...])


def _cross(x, g, k, v, wq, wo):
    bsz, seq, d = x.shape
    mlen = k.shape[1]
    tile = pl.BlockSpec((1, TOKEN_TILE, d), lambda b, i: (b, i, 0))
    kv = pl.BlockSpec((1, mlen, d), lambda b, i: (b, 0, 0))
    return pl.pallas_call(
        _cross_kernel,
        grid=(bsz, seq // TOKEN_TILE),
        in_specs=[tile, _resident((1, d)), kv, kv, _resident(wq.shape), _resident(wo.shape)],
        out_specs=tile,
        out_shape=jax.ShapeDtypeStruct((bsz, seq, d), F32),
        compiler_params=_params("parallel", "arbitrary"),
        name="cross",
    )(x, g, k, v, wq, wo)


def kernel(x, mem, norm_ffn1, ffn1_w_gate, ffn1_w_up, ffn1_w_down, norm_mix, w_in, conv_dw_w, conv_dw_b, conv_ln_g, conv_ln_b, conv_w_out, win_sink, t5_bias, win_w_out, na_rpb, na_w_out, w_out, norm_cross, norm_mem, cross_w_q, cross_w_kv, cross_w_o, norm_ffn2, ffn2_w_gate, ffn2_w_up, ffn2_w_down, norm_final):
    bsz, seq, d = x.shape
    depth = w_in.shape[0]
    t = bsz * seq
    assert d == D_MODEL and t % TOKEN_TILE == 0 and seq % TOKEN_TILE == 0
    assert seq % (NA_Q_ROWS * GRID_W) == 0 and seq // GRID_W >= NA_ROWS and seq % CONV_ROWS == 0

    def row(v):
        return v.reshape(1, -1).astype(F32)

    def bf(w):
        return w.astype(BF16)

    win_bias = _win_bias(t5_bias)
    gf = row(norm_final)
    xt = x.reshape(t, d)
    for l in range(depth):
        xt = _ffn(xt, row(norm_ffn1[l]), bf(ffn1_w_gate[l]), bf(ffn1_w_up[l]), bf(ffn1_w_down[l]),
                  gf, False)
        z, bq, bk, bv, cq, ck, cv, gates = _inproj(xt, row(norm_mix[l]), bf(w_in[l]))

        def seqs(a):
            return a.reshape(bsz, seq, a.shape[-1])

        za = _conv(seqs(z), conv_dw_w[l], row(conv_dw_b[l]), row(conv_ln_g[l]), row(conv_ln_b[l]))
        ob = _win_attn(seqs(bq), seqs(bk), seqs(bv), win_sink[l].astype(F32), win_bias)
        oc = _na_attn(seqs(cq), seqs(ck), seqs(cv), _na_bias(na_rpb[l]))
        xt = _mixout(xt, za.reshape(t, -1), ob.reshape(t, -1), oc.reshape(t, -1), gates,
                     bf(conv_w_out[l]), bf(win_w_out[l]), bf(na_w_out[l]), bf(w_out[l]))
        mk, mv = _memkv(mem, row(norm_mem[l]), bf(cross_w_kv[l]))
        xt = _cross(xt.reshape(bsz, seq, d), row(norm_cross[l]), mk, mv,
                    bf(cross_w_q[l]), bf(cross_w_o[l])).reshape(t, d)
        xt = _ffn(xt, row(norm_ffn2[l]), bf(ffn2_w_gate[l]), bf(ffn2_w_up[l]), bf(ffn2_w_down[l]),
                  gf, l == depth - 1)
    return xt.reshape(bsz, seq, d)
```

```python
import functools

import numpy as np
import jax
import jax.numpy as jnp
from jax import lax
from jax.experimental import pallas as pl
from jax.experimental.pallas import tpu as pltpu

F32 = jnp.float32
BF16 = jnp.bfloat16

D_MODEL = 1024
EPS = 1e-6
NEG_INF = -1e30
HEAD_DIM = 64
D_FF = 11 * D_MODEL // 4
CONV_CH = D_MODEL // 4
CONV_WIDTH = 31
WIN_HEADS = (D_MODEL // 2) // HEAD_DIM
WIN_KV_HEADS = 2
WIN_GROUP = WIN_HEADS // WIN_KV_HEADS
WINDOW = 128
BLOCK = 128
T5_BUCKETS = 32
T5_MAX_DIST = 128
NA_HEADS = (D_MODEL // 4) // HEAD_DIM
NA_ROWS = 8
NA_COLS = 16
GRID_W = 64
X_HEADS = 4
X_HEAD_DIM = D_MODEL // X_HEADS
N_BRANCH = 3

WIN_Q = WIN_HEADS * HEAD_DIM
WIN_KV = WIN_KV_HEADS * HEAD_DIM
NA_W = NA_HEADS * HEAD_DIM
OFF_CONV = 0
OFF_BQ = OFF_CONV + 2 * CONV_CH
OFF_BK = OFF_BQ + WIN_Q
OFF_BV = OFF_BK + WIN_KV
OFF_CQ = OFF_BV + WIN_KV
OFF_CK = OFF_CQ + NA_W
OFF_CV = OFF_CK + NA_W
OFF_GATE = OFF_CV + NA_W
IN_WIDTH = OFF_GATE + N_BRANCH * D_MODEL

VMEM_LIMIT_BYTES = 56 * 1024 * 1024
TOKEN_TILE = 512
FF_CHUNK = 256
CONV_ROWS = 128
CONV_PAD = 16
NA_Q_ROWS = 8


def _params(*sem, flags=None):
    return pltpu.CompilerParams(dimension_semantics=sem, vmem_limit_bytes=VMEM_LIMIT_BYTES, flags=flags)


def _resident(shape):
    zeros = (0,) * len(shape)
    return pl.BlockSpec(shape, lambda *_: zeros, pipeline_mode=pl.Buffered(1))


def _rms(x, g):
    ms = jnp.mean(x * x, axis=-1, keepdims=True)
    return x * lax.rsqrt(ms + EPS) * g


def _dot(a, b):
    return jnp.dot(a, b, preferred_element_type=F32)


def _dot_nt(a, b):
    return lax.dot_general(a, b, (((1,), (1,)), ((), ())), preferred_element_type=F32)


def _ffn_kernel(x_ref, g_ref, wg_ref, wu_ref, wd_ref, gf_ref, o_ref, act_ref, *, final_norm):
    x = x_ref[...]
    h = _rms(x, g_ref[...]).astype(BF16)
    for c in range(D_FF // FF_CHUNK):
        sl = slice(c * FF_CHUNK, (c + 1) * FF_CHUNK)
        a = _dot(h, wg_ref[:, sl])
        b = _dot(h, wu_ref[:, sl])
        act_ref[:, sl] = (a * jax.nn.sigmoid(a) * b).astype(BF16)
    y = x + 0.5 * _dot(act_ref[...], wd_ref[...])
    if final_norm:
        y = _rms(y, gf_ref[...])
    o_ref[...] = y


def _ffn(x, g, wg, wu, wd, gf, final_norm):
    t, d = x.shape
    tile = pl.BlockSpec((TOKEN_TILE, d), lambda i: (i, 0))
    return pl.pallas_call(
        functools.partial(_ffn_kernel, final_norm=final_norm),
        grid=(t // TOKEN_TILE,),
        in_specs=[tile, _resident((1, d)), _resident((d, D_FF)), _resident((d, D_FF)),
                  _resident((D_FF, d)), _resident((1, d))],
        out_specs=tile,
        out_shape=jax.ShapeDtypeStruct((t, d), F32),
        scratch_shapes=[pltpu.VMEM((TOKEN_TILE, D_FF), BF16)],
        compiler_params=_params("parallel"),
        name="ffn",
    )(x, g, wg, wu, wd, gf)


def _inproj_kernel(x_ref, g_ref, w_ref, z_ref, bq_ref, bk_ref, bv_ref, cq_ref, ck_ref, cv_ref, gate_ref):
    h = _rms(x_ref[...], g_ref[...]).astype(BF16)
    scale = HEAD_DIM ** -0.5
    ua = _dot(h, w_ref[:, OFF_CONV:OFF_BQ])
    z_ref[...] = ua[:, :CONV_CH] * jax.nn.sigmoid(ua[:, CONV_CH:])
    bq_ref[...] = (_dot(h, w_ref[:, OFF_BQ:OFF_BK]) * scale).astype(BF16)
    bkv = _dot(h, w_ref[:, OFF_BK:OFF_CQ])
    bk_ref[...] = bkv[:, :WIN_KV].astype(BF16)
    bv_ref[...] = bkv[:, WIN_KV:].astype(BF16)
    cq_ref[...] = (_dot(h, w_ref[:, OFF_CQ:OFF_CK]) * scale).astype(BF16)
    ck_ref[...] = _dot(h, w_ref[:, OFF_CK:OFF_CV]).astype(BF16)
    cv_ref[...] = _dot(h, w_ref[:, OFF_CV:OFF_GATE]).astype(BF16)
    for c in range(N_BRANCH):
        lo = OFF_GATE + c * D_MODEL
        gate_ref[:, c * D_MODEL:(c + 1) * D_MODEL] = _dot(h, w_ref[:, lo:lo + D_MODEL]).astype(BF16)


def _inproj(x, g, w):
    t, d = x.shape
    widths = (CONV_CH, WIN_Q, WIN_KV, WIN_KV, NA_W, NA_W, NA_W, N_BRANCH * D_MODEL)
    dtypes = (F32,) + (BF16,) * 7

    def tile(n):
        return pl.BlockSpec((TOKEN_TILE, n), lambda i: (i, 0))

    return pl.pallas_call(
        _inproj_kernel,
        grid=(t // TOKEN_TILE,),
        in_specs=[tile(d), _resident((1, d)), _resident((d, IN_WIDTH))],
        out_specs=[tile(n) for n in widths],
        out_shape=[jax.ShapeDtypeStruct((t, n), dt) for n, dt in zip(widths, dtypes)],
        compiler_params=_params("parallel"),
        name="inproj",
    )(x, g, w)


def _conv_kernel(z_ref, w_ref, b_ref, lg_ref, lb_ref, o_ref, zp_ref, *, seq):
    zeros = jnp.zeros((CONV_PAD, CONV_CH), F32)
    zp_ref[0:CONV_PAD, :] = zeros
    zp_ref[CONV_PAD + seq:CONV_PAD + seq + CONV_PAD, :] = zeros
    zp_ref[CONV_PAD:CONV_PAD + seq, :] = z_ref[0]
    w = w_ref[...]
    half = CONV_WIDTH // 2
    lead = CONV_PAD - half
    n_shift = 8
    span = CONV_ROWS + 2 * CONV_PAD - n_shift

    def chunk(c, carry):
        base = pl.multiple_of(c * CONV_ROWS, CONV_ROWS)
        slab = zp_ref[pl.ds(base, CONV_ROWS + 2 * CONV_PAD), :]
        acc = jnp.zeros((CONV_ROWS, CONV_CH), F32) + b_ref[...]
        for s in range(n_shift):
            shifted = slab[s:s + span, :]
            for a in range(0, span - CONV_ROWS + 1, n_shift):
                k = a + s - lead
                if 0 <= k < CONV_WIDTH:
                    acc = acc + shifted[a:a + CONV_ROWS, :] * w[k:k + 1, :]
        mu = jnp.mean(acc, axis=-1, keepdims=True)
        cen = acc - mu
        var = jnp.mean(cen * cen, axis=-1, keepdims=True)
        y = cen * lax.rsqrt(var + EPS) * lg_ref[...] + lb_ref[...]
        o_ref[0, pl.ds(base, CONV_ROWS), :] = (y * jax.nn.sigmoid(y)).astype(BF16)
        return carry

    lax.fori_loop(0, seq // CONV_ROWS, chunk, 0)


def _conv(z, w, b, lg, lb):
    bsz, seq, ch = z.shape
    blk = pl.BlockSpec((1, seq, ch), lambda i: (i, 0, 0))
    return pl.pallas_call(
        functools.partial(_conv_kernel, seq=seq),
        grid=(bsz,),
        in_specs=[blk, _resident((CONV_WIDTH, ch)), _resident((1, ch)), _resident((1, ch)),
                  _resident((1, ch))],
        out_specs=blk,
        out_shape=jax.ShapeDtypeStruct((bsz, seq, ch), BF16),
        scratch_shapes=[pltpu.VMEM((seq + 2 * CONV_PAD, ch), F32)],
        compiler_params=_params("parallel"),
        name="conv",
    )(z, w, b, lg, lb)


def _t5_buckets(rel):
    half = T5_BUCKETS // 2
    max_exact = half // 2
    ret = (rel > 0).astype(np.int32) * half
    n = np.abs(rel)
    large = max_exact + (np.log(np.maximum(n, 1) / max_exact)
                         / np.log(T5_MAX_DIST / max_exact) * (half - max_exact)).astype(np.int32)
    large = np.minimum(large, half - 1)
    return ret + np.where(n < max_exact, n, large)


WIN_PAIRS = WIN_HEADS // WIN_KV_HEADS
WIN_SPAN = BLOCK + 2 * WINDOW
WIN_QB = 4
assert WIN_KV_HEADS == 2 and 2 * HEAD_DIM == 128


def _win_head_perm():
    heads = [h for t in range(WIN_PAIRS) for h in (t, t + WIN_GROUP)]
    return np.concatenate([np.arange(h * HEAD_DIM, (h + 1) * HEAD_DIM) for h in heads])


def _win_bias(t5_table):
    n = WIN_SPAN + BLOCK
    rel = np.arange(n) - (n // 2 - 1)
    line = jnp.where((np.abs(rel) <= WINDOW)[:, None], t5_table[_t5_buckets(rel)].astype(F32), NEG_INF)
    line = jnp.transpose(line)
    skew = jnp.tile(line, (1, BLOCK))[:, :BLOCK * (n - 1)].reshape(WIN_HEADS, BLOCK, n - 1)
    bias = skew[:, :, BLOCK - 1:BLOCK - 1 + WIN_SPAN]
    bias_t = jnp.transpose(bias, (0, 2, 1))
    key = np.arange(WIN_SPAN)[None, :, None]
    first = jnp.where(key < WINDOW, NEG_INF, bias_t)
    last = jnp.where(key >= WINDOW + BLOCK, NEG_INF, bias_t)
    return jnp.stack([first, bias_t, last])


def _win_kernel(sink_ref, q_ref, kp_ref, kc_ref, kn_ref, vp_ref, vc_ref, vn_ref, bias_ref, o_ref,
                k_scr, vt_scr, *, nblk):
    i = pl.program_id(1)
    k_scr[0] = kp_ref[0]
    vt_scr[0] = vp_ref[0]
    for b in range(WIN_QB):
        k_scr[1 + b] = kc_ref[0, b * BLOCK:(b + 1) * BLOCK, :]
        vt_scr[1 + b] = vc_ref[0, :, b * BLOCK:(b + 1) * BLOCK]
    k_scr[WIN_QB + 1] = kn_ref[0]
    vt_scr[WIN_QB + 1] = vn_ref[0]

    def block(jb, carry):
        g = i * WIN_QB + jb
        variant = jnp.where(g == 0, 0, jnp.where(g == nblk - 1, 2, 1))
        rows = pl.ds(pl.multiple_of(jb * BLOCK, BLOCK), BLOCK)
        k = jnp.concatenate([k_scr[jb], k_scr[jb + 1], k_scr[jb + 2]], axis=0)
        vt = jnp.concatenate([vt_scr[jb], vt_scr[jb + 1], vt_scr[jb + 2]], axis=1)
        low = lax.broadcasted_iota(jnp.int32, k.shape, 1) < HEAD_DIM
        k_half = (jnp.where(low, k, 0), jnp.where(low, 0, k))
        ch = lax.broadcasted_iota(jnp.int32, vt.shape, 0)
        den_row = (HEAD_DIM, 0)
        vt_half = (jnp.where(ch < HEAD_DIM, vt, jnp.where(ch == den_row[0], 1, 0).astype(BF16)),
                   jnp.where(ch >= HEAD_DIM, vt, jnp.where(ch == den_row[1], 1, 0).astype(BF16)))
        for t in range(WIN_PAIRS):
            qp = q_ref[0, rows, t * 128:(t + 1) * 128]
            halves = []
            for j in range(WIN_KV_HEADS):
                h = t + j * WIN_GROUP
                sink = sink_ref[h]
                s = _dot_nt(k_half[j], qp) + bias_ref[variant, h]
                m = jnp.maximum(jnp.max(s, axis=0, keepdims=True), sink)
                p = jnp.exp(s - m).astype(BF16)
                o = _dot(vt_half[j], p)
                den = o[den_row[j]:den_row[j] + 1] + jnp.exp(sink - m)
                halves.append(o[j * HEAD_DIM:(j + 1) * HEAD_DIM] * (1.0 / den))
            o_ref[0, rows, t * 128:(t + 1) * 128] = jnp.concatenate(halves, axis=0).T.astype(BF16)
        return carry

    lax.fori_loop(0, WIN_QB, block, 0, unroll=True)


def _win_attn(q, k, vt, sink, bias):
    bsz, seq, _ = q.shape
    nblk = seq // BLOCK
    assert nblk >= 2 and nblk % WIN_QB == 0

    def edge(shift):
        return lambda b, i: jnp.clip(i * WIN_QB + shift, 0, nblk - 1)

    def k_edge(shift):
        return pl.BlockSpec((1, BLOCK, WIN_KV), lambda b, i: (b, edge(shift)(b, i), 0))

    def v_edge(shift):
        return pl.BlockSpec((1, WIN_KV, BLOCK), lambda b, i: (b, 0, edge(shift)(b, i)))

    qspec = pl.BlockSpec((1, WIN_QB * BLOCK, WIN_Q), lambda b, i: (b, i, 0))
    return pl.pallas_call(
        functools.partial(_win_kernel, nblk=nblk),
        grid=(bsz, nblk // WIN_QB),
        in_specs=[pl.BlockSpec(memory_space=pltpu.SMEM), qspec,
                  k_edge(-1), pl.BlockSpec((1, WIN_QB * BLOCK, WIN_KV), lambda b, i: (b, i, 0)), k_edge(WIN_QB),
                  v_edge(-1), pl.BlockSpec((1, WIN_KV, WIN_QB * BLOCK), lambda b, i: (b, 0, i)), v_edge(WIN_QB),
                  _resident(bias.shape)],
        out_specs=qspec,
        out_shape=jax.ShapeDtypeStruct((bsz, seq, WIN_Q), BF16),
        scratch_shapes=[pltpu.VMEM((WIN_QB + 2, BLOCK, WIN_KV), BF16),
                        pltpu.VMEM((WIN_QB + 2, WIN_KV, BLOCK), BF16)],
        compiler_params=_params("parallel", "parallel"),
        name="win_attn",
    )(sink, q, k, k, k, vt, vt, vt, bias)


def _na_bias(rpb):
    c = np.arange(GRID_W)
    col_start = np.clip(c - NA_COLS // 2, 0, GRID_W - NA_COLS)
    col = np.arange(GRID_W)
    inside = (col[None, :] >= col_start[:, None]) & (col[None, :] < col_start[:, None] + NA_COLS)
    rpb = rpb.astype(F32)
    b = jnp.stack([rpb[:, NA_ROWS - 1 - p:2 * NA_ROWS - 1 - p] for p in range(NA_ROWS)], axis=1)
    n = 2 * GRID_W
    lead = GRID_W - NA_COLS
    line = jnp.pad(b, ((0, 0),) * 3 + ((lead, n - lead - (2 * NA_COLS - 1)),))
    skew = jnp.tile(line, (1, 1, 1, GRID_W))[..., :GRID_W * (n - 1)]
    skew = skew.reshape(b.shape[:3] + (GRID_W, n - 1))[..., GRID_W - 1:2 * GRID_W - 1]
    b = jnp.where(inside[None, None, None], skew, NEG_INF)
    b = b.reshape((NA_HEADS // 2, 2) + b.shape[1:])
    b = jnp.transpose(b, (2, 0, 3, 5, 1, 4))
    return b.reshape(NA_ROWS, NA_HEADS // 2, NA_ROWS * GRID_W, 2 * GRID_W)


def _na_kernel(q_ref, k_ref, v_ref, bias_ref, o_ref, *, rows):
    j = pl.program_id(1)
    win = NA_ROWS * GRID_W
    low = lax.broadcasted_iota(jnp.int32, (GRID_W, 2 * HEAD_DIM), 1) < HEAD_DIM

    def row(t, carry):
        r = j * NA_Q_ROWS + t
        rs = jnp.clip(r - NA_ROWS // 2, 0, rows - NA_ROWS)
        keys = pl.ds(pl.multiple_of(rs * GRID_W, GRID_W), win)
        qrows = pl.ds(pl.multiple_of(t * GRID_W, GRID_W), GRID_W)
        for tile in range(NA_HEADS // 2):
            lanes = slice(tile * 2 * HEAD_DIM, (tile + 1) * 2 * HEAD_DIM)
            q = q_ref[0, qrows, lanes]
            q_bd = jnp.concatenate([jnp.where(low, q, 0), jnp.where(low, 0, q)], axis=0)
            s = _dot_nt(k_ref[0, keys, lanes], q_bd) + bias_ref[r - rs, tile]
            m = jnp.max(s, axis=0, keepdims=True)
            p = jnp.exp(s - m)
            den = jnp.sum(p, axis=0, keepdims=True)
            o = lax.dot_general(v_ref[0, keys, lanes], p.astype(BF16), (((0,), (0,)), ((), ())),
                                preferred_element_type=F32)
            ot = (o * (1.0 / den)).T
            o_ref[0, qrows, lanes] = jnp.where(low, ot[:GRID_W], ot[GRID_W:]).astype(BF16)
        return carry

    lax.fori_loop(0, NA_Q_ROWS, row, 0, unroll=True)


def _na_attn(q, k, v, bias):
    bsz, seq, _ = q.shape
    rows = seq // GRID_W
    qspec = pl.BlockSpec((1, NA_Q_ROWS * GRID_W, NA_W), lambda b, j: (b, j, 0))
    kvspec = pl.BlockSpec((1, seq, NA_W), lambda b, j: (b, 0, 0))
    return pl.pallas_call(
        functools.partial(_na_kernel, rows=rows),
        grid=(bsz, rows // NA_Q_ROWS),
        in_specs=[qspec, kvspec, kvspec, _resident(bias.shape)],
        out_specs=qspec,
        out_shape=jax.ShapeDtypeStruct((bsz, seq, NA_W), BF16),
        compiler_params=_params("parallel", "arbitrary"),
        name="na_attn",
    )(q, k, v, bias)


def _mixout_kernel(x_ref, za_ref, ob_ref, oc_ref, gate_ref, wa_ref, wb_ref, wc_ref, wo_ref, o_ref):
    def gate(c):
        return jax.nn.sigmoid(gate_ref[:, c * D_MODEL:(c + 1) * D_MODEL].astype(F32))

    y = gate(0) * _dot(za_ref[...], wa_ref[...])
    y = y + gate(1) * _dot(ob_ref[...], wb_ref[...])
    y = y + gate(2) * _dot(oc_ref[...], wc_ref[...])
    o_ref[...] = x_ref[...] + _dot(y.astype(BF16), wo_ref[...])


def _mixout(x, za, ob, oc, gates, wa, wb, wc, wo):
    t, d = x.shape

    def tile(n):
        return pl.BlockSpec((TOKEN_TILE, n), lambda i: (i, 0))

    return pl.pallas_call(
        _mixout_kernel,
        grid=(t // TOKEN_TILE,),
        in_specs=[tile(d), tile(CONV_CH), tile(WIN_Q), tile(NA_W), tile(N_BRANCH * d),
                  _resident(wa.shape), _resident(wb.shape), _resident(wc.shape), _resident(wo.shape)],
        out_specs=tile(d),
        out_shape=jax.ShapeDtypeStruct((t, d), F32),
        compiler_params=_params("parallel"),
        name="mixout",
    )(x, za, ob, oc, gates, wa, wb, wc, wo)


def _memkv_kernel(m_ref, g_ref, w_ref, k_ref, v_ref):
    h = _rms(m_ref[0], g_ref[...]).astype(BF16)
    k_ref[0] = _dot(h, w_ref[:, :D_MODEL]).astype(BF16)
    v_ref[0] = _dot(h, w_ref[:, D_MODEL:]).astype(BF16)


def _memkv(mem, g, w):
    bsz, mlen, d = mem.shape
    blk = pl.BlockSpec((1, mlen, d), lambda b: (b, 0, 0))
    return pl.pallas_call(
        _memkv_kernel,
        grid=(bsz,),
        in_specs=[blk, _resident((1, d)), _resident(w.shape)],
        out_specs=[blk, blk],
        out_shape=[jax.ShapeDtypeStruct((bsz, mlen, d), BF16)] * 2,
        compiler_params=_params("parallel"),
        name="memkv",
    )(mem, g, w)


def _cross_kernel(x_ref, g_ref, k_ref, v_ref, wq_ref, wo_ref, o_ref):
    x = x_ref[0]
    h = _rms(x, g_ref[...]).astype(BF16)
    scale = X_HEAD_DIM ** -0.5
    q = (_dot(h, wq_ref[...]) * scale).astype(BF16)
    k = k_ref[0]
    v = v_ref[0]
    outs = []
    for hd in range(X_HEADS):
        hs = slice(hd * X_HEAD_DIM, (hd + 1) * X_HEAD_DIM)
        s = _dot_nt(q[:, hs], k[:, hs])
        m = jnp.max(s, axis=-1, keepdims=True)
        p = jnp.exp(s - m)
        den = jnp.sum(p, axis=-1, keepdims=True)
        outs.append((_dot(p.astype(BF16), v[:, hs]) / den).astype(BF16))
    o = jnp.concatenate(outs, axis=-1)
    o_ref[0] = x + _dot(o, wo_ref[...])


def _cross(x, g, k, v, wq, wo):
    bsz, seq, d = x.shape
    mlen = k.shape[1]
    tile = pl.BlockSpec((1, TOKEN_TILE, d), lambda b, i: (b, i, 0))
    kv = pl.BlockSpec((1, mlen, d), lambda b, i: (b, 0, 0))
    return pl.pallas_call(
        _cross_kernel,
        grid=(bsz, seq // TOKEN_TILE),
        in_specs=[tile, _resident((1, d)), kv, kv, _resident(wq.shape), _resident(wo.shape)],
        out_specs=tile,
        out_shape=jax.ShapeDtypeStruct((bsz, seq, d), F32),
        compiler_params=_params("parallel", "arbitrary"),
        name="cross",
    )(x, g, k, v, wq, wo)


def kernel(x, mem, norm_ffn1, ffn1_w_gate, ffn1_w_up, ffn1_w_down, norm_mix, w_in, conv_dw_w, conv_dw_b, conv_ln_g, conv_ln_b, conv_w_out, win_sink, t5_bias, win_w_out, na_rpb, na_w_out, w_out, norm_cross, norm_mem, cross_w_q, cross_w_kv, cross_w_o, norm_ffn2, ffn2_w_gate, ffn2_w_up, ffn2_w_down, norm_final):
    bsz, seq, d = x.shape
    depth = w_in.shape[0]
    t = bsz * seq
    assert d == D_MODEL and t % TOKEN_TILE == 0 and seq % TOKEN_TILE == 0
    assert seq % (NA_Q_ROWS * GRID_W) == 0 and seq // GRID_W >= NA_ROWS and seq % CONV_ROWS == 0

    def row(v):
        return v.reshape(1, -1).astype(F32)

    def bf(w):
        return w.astype(BF16)

    win_bias = _win_bias(t5_bias)
    win_heads = [h for tl in range(WIN_PAIRS) for h in (tl, tl + WIN_GROUP)]
    gf = row(norm_final)
    xt = x.reshape(t, d)
    for l in range(depth):
        xt = _ffn(xt, row(norm_ffn1[l]), bf(ffn1_w_gate[l]), bf(ffn1_w_up[l]), bf(ffn1_w_down[l]),
                  gf, False)
        w_in_l = jnp.concatenate(
            [w_in[l][:, :OFF_BQ]]
            + [w_in[l][:, OFF_BQ + h * HEAD_DIM:OFF_BQ + (h + 1) * HEAD_DIM] for h in win_heads]
            + [w_in[l][:, OFF_BK:]], axis=1)
        win_w_out_l = jnp.concatenate(
            [win_w_out[l][h * HEAD_DIM:(h + 1) * HEAD_DIM] for h in win_heads], axis=0)
        z, bq, bk, bv, cq, ck, cv, gates = _inproj(xt, row(norm_mix[l]), bf(w_in_l))

        def seqs(a):
            return a.reshape(bsz, seq, a.shape[-1])

        za = _conv(seqs(z), conv_dw_w[l], row(conv_dw_b[l]), row(conv_ln_g[l]), row(conv_ln_b[l]))
        ob = _win_attn(seqs(bq), seqs(bk), jnp.swapaxes(seqs(bv), 1, 2), win_sink[l].astype(F32), win_bias)
        oc = _na_attn(seqs(cq), seqs(ck), seqs(cv), _na_bias(na_rpb[l]))
        xt = _mixout(xt, za.reshape(t, -1), ob.reshape(t, -1), oc.reshape(t, -1), gates,
                     bf(conv_w_out[l]), bf(win_w_out_l), bf(na_w_out[l]), bf(w_out[l]))
        mk, mv = _memkv(mem, row(norm_mem[l]), bf(cross_w_kv[l]))
        xt = _cross(xt.reshape(bsz, seq, d), row(norm_cross[l]), mk, mv,
                    bf(cross_w_q[l]), bf(cross_w_o[l])).reshape(t, d)
        xt = _ffn(xt, row(norm_ffn2[l]), bf(ffn2_w_gate[l]), bf(ffn2_w_up[l]), bf(ffn2_w_down[l]),
                  gf, l == depth - 1)
    return xt.reshape(bsz, seq, d)
```

```python
import functools

import numpy as np
import jax
import jax.numpy as jnp
from jax import lax
from jax.experimental import pallas as pl
from jax.experimental.pallas import tpu as pltpu

F32 = jnp.float32
BF16 = jnp.bfloat16

D_MODEL = 1024
EPS = 1e-6
NEG_INF = -1e30
HEAD_DIM = 64
D_FF = 11 * D_MODEL // 4
CONV_CH = D_MODEL // 4
CONV_WIDTH = 31
WIN_HEADS = (D_MODEL // 2) // HEAD_DIM
WIN_KV_HEADS = 2
WIN_GROUP = WIN_HEADS // WIN_KV_HEADS
WINDOW = 128
BLOCK = 128
T5_BUCKETS = 32
T5_MAX_DIST = 128
NA_HEADS = (D_MODEL // 4) // HEAD_DIM
NA_ROWS = 8
NA_COLS = 16
GRID_W = 64
X_HEADS = 4
X_HEAD_DIM = D_MODEL // X_HEADS
N_BRANCH = 3

WIN_Q = WIN_HEADS * HEAD_DIM
WIN_KV = WIN_KV_HEADS * HEAD_DIM
NA_W = NA_HEADS * HEAD_DIM
OFF_CONV = 0
OFF_BQ = OFF_CONV + 2 * CONV_CH
OFF_BK = OFF_BQ + WIN_Q
OFF_BV = OFF_BK + WIN_KV
OFF_CQ = OFF_BV + WIN_KV
OFF_CK = OFF_CQ + NA_W
OFF_CV = OFF_CK + NA_W
OFF_GATE = OFF_CV + NA_W
IN_WIDTH = OFF_GATE + N_BRANCH * D_MODEL

VMEM_LIMIT_BYTES = 56 * 1024 * 1024
TOKEN_TILE = 512
FF_CHUNK = 256
CONV_ROWS = 128
CONV_PAD = 16
NA_Q_ROWS = 8


def _params(*sem, flags=None):
    return pltpu.CompilerParams(dimension_semantics=sem, vmem_limit_bytes=VMEM_LIMIT_BYTES, flags=flags)


def _resident(shape):
    zeros = (0,) * len(shape)
    return pl.BlockSpec(shape, lambda *_: zeros, pipeline_mode=pl.Buffered(1))


def _rms(x, g):
    ms = jnp.mean(x * x, axis=-1, keepdims=True)
    return x * lax.rsqrt(ms + EPS) * g


def _dot(a, b):
    return jnp.dot(a, b, preferred_element_type=F32)


def _dot_nt(a, b):
    return lax.dot_general(a, b, (((1,), (1,)), ((), ())), preferred_element_type=F32)


def _ffn_kernel(x_ref, g_ref, wg_ref, wu_ref, wd_ref, gf_ref, o_ref, act_ref, *, final_norm):
    x = x_ref[...]
    h = _rms(x, g_ref[...]).astype(BF16)
    for c in range(D_FF // FF_CHUNK):
        sl = slice(c * FF_CHUNK, (c + 1) * FF_CHUNK)
        a = _dot(h, wg_ref[:, sl])
        b = _dot(h, wu_ref[:, sl])
        act_ref[:, sl] = (a * jax.nn.sigmoid(a) * b).astype(BF16)
    y = x + 0.5 * _dot(act_ref[...], wd_ref[...])
    if final_norm:
        y = _rms(y, gf_ref[...])
    o_ref[...] = y


def _ffn(x, g, wg, wu, wd, gf, final_norm):
    t, d = x.shape
    tile = pl.BlockSpec((TOKEN_TILE, d), lambda i: (i, 0))
    return pl.pallas_call(
        functools.partial(_ffn_kernel, final_norm=final_norm),
        grid=(t // TOKEN_TILE,),
        in_specs=[tile, _resident((1, d)), _resident((d, D_FF)), _resident((d, D_FF)),
                  _resident((D_FF, d)), _resident((1, d))],
        out_specs=tile,
        out_shape=jax.ShapeDtypeStruct((t, d), F32),
        scratch_shapes=[pltpu.VMEM((TOKEN_TILE, D_FF), BF16)],
        compiler_params=_params("parallel"),
        name="ffn",
    )(x, g, wg, wu, wd, gf)


def _inproj_kernel(x_ref, g_ref, w_ref, wq_ref, z_ref, bq_ref, bk_ref, bv_ref, cq_ref, ck_ref, cv_ref):
    h = _rms(x_ref[...], g_ref[...]).astype(BF16)
    scale = HEAD_DIM ** -0.5
    ua = _dot(h, w_ref[:, OFF_CONV:OFF_BQ])
    z_ref[...] = ua[:, :CONV_CH] * jax.nn.sigmoid(ua[:, CONV_CH:])
    bq_ref[...] = (_dot(h, wq_ref[...]) * scale).astype(BF16)
    bkv = _dot(h, w_ref[:, OFF_BK:OFF_CQ])
    bk_ref[...] = bkv[:, :WIN_KV].astype(BF16)
    bv_ref[...] = bkv[:, WIN_KV:].astype(BF16)
    cq_ref[...] = (_dot(h, w_ref[:, OFF_CQ:OFF_CK]) * scale).astype(BF16)
    ck_ref[...] = _dot(h, w_ref[:, OFF_CK:OFF_CV]).astype(BF16)
    cv_ref[...] = _dot(h, w_ref[:, OFF_CV:OFF_GATE]).astype(BF16)


def _inproj(x, g, w, wq):
    t, d = x.shape
    widths = (CONV_CH, WIN_Q, WIN_KV, WIN_KV, NA_W, NA_W, NA_W)
    dtypes = (F32,) + (BF16,) * 6

    def tile(n):
        return pl.BlockSpec((TOKEN_TILE, n), lambda i: (i, 0))

    return pl.pallas_call(
        _inproj_kernel,
        grid=(t // TOKEN_TILE,),
        in_specs=[tile(d), _resident((1, d)), _resident((d, OFF_GATE)), _resident(wq.shape)],
        out_specs=[tile(n) for n in widths],
        out_shape=[jax.ShapeDtypeStruct((t, n), dt) for n, dt in zip(widths, dtypes)],
        compiler_params=_params("parallel"),
        name="inproj",
    )(x, g, w, wq)


def _conv_kernel(z_ref, w_ref, b_ref, lg_ref, lb_ref, o_ref, zp_ref, *, seq):
    zeros = jnp.zeros((CONV_PAD, CONV_CH), F32)
    zp_ref[0:CONV_PAD, :] = zeros
    zp_ref[CONV_PAD + seq:CONV_PAD + seq + CONV_PAD, :] = zeros
    zp_ref[CONV_PAD:CONV_PAD + seq, :] = z_ref[0]
    w = w_ref[...]
    half = CONV_WIDTH // 2
    lead = CONV_PAD - half
    n_shift = 8
    slab_rows = CONV_ROWS + 2 * CONV_PAD
    span = slab_rows - n_shift

    def chunk(c, carry):
        base = pl.multiple_of(c * CONV_ROWS, CONV_ROWS)
        slab = zp_ref[pl.ds(base, slab_rows), :]
        acc = jnp.zeros((CONV_ROWS, CONV_CH), F32) + b_ref[...]
        for s in range(n_shift):
            shifted = slab if s == 0 else pltpu.roll(slab, slab_rows - s, axis=0)
            for a in range(0, span - CONV_ROWS + 1, n_shift):
                k = a + s - lead
                if 0 <= k < CONV_WIDTH:
                    acc = acc + shifted[a:a + CONV_ROWS, :] * w[k:k + 1, :]
        mu = jnp.mean(acc, axis=-1, keepdims=True)
        cen = acc - mu
        var = jnp.mean(cen * cen, axis=-1, keepdims=True)
        y = cen * lax.rsqrt(var + EPS) * lg_ref[...] + lb_ref[...]
        o_ref[0, pl.ds(base, CONV_ROWS), :] = (y * jax.nn.sigmoid(y)).astype(BF16)
        return carry

    lax.fori_loop(0, seq // CONV_ROWS, chunk, 0, unroll=2)


def _conv(z, w, b, lg, lb):
    bsz, seq, ch = z.shape
    blk = pl.BlockSpec((1, seq, ch), lambda i: (i, 0, 0))
    return pl.pallas_call(
        functools.partial(_conv_kernel, seq=seq),
        grid=(bsz,),
        in_specs=[blk, _resident((CONV_WIDTH, ch)), _resident((1, ch)), _resident((1, ch)),
                  _resident((1, ch))],
        out_specs=blk,
        out_shape=jax.ShapeDtypeStruct((bsz, seq, ch), BF16),
        scratch_shapes=[pltpu.VMEM((seq + 2 * CONV_PAD, ch), F32)],
        compiler_params=_params("parallel"),
        name="conv",
    )(z, w, b, lg, lb)


def _t5_buckets(rel):
    half = T5_BUCKETS // 2
    max_exact = half // 2
    ret = (rel > 0).astype(np.int32) * half
    n = np.abs(rel)
    large = max_exact + (np.log(np.maximum(n, 1) / max_exact)
                         / np.log(T5_MAX_DIST / max_exact) * (half - max_exact)).astype(np.int32)
    large = np.minimum(large, half - 1)
    return ret + np.where(n < max_exact, n, large)


WIN_PAIRS = WIN_HEADS // WIN_KV_HEADS
WIN_SPAN = BLOCK + 2 * WINDOW
WIN_QB = 4
assert WIN_KV_HEADS == 2 and 2 * HEAD_DIM == 128


def _win_head_perm():
    heads = [h for t in range(WIN_PAIRS) for h in (t, t + WIN_GROUP)]
    return np.concatenate([np.arange(h * HEAD_DIM, (h + 1) * HEAD_DIM) for h in heads])


def _win_bias(t5_table):
    n = WIN_SPAN + BLOCK
    rel = np.arange(n) - (n // 2 - 1)
    line = jnp.where((np.abs(rel) <= WINDOW)[:, None], t5_table[_t5_buckets(rel)].astype(F32), NEG_INF)
    line = jnp.transpose(line)
    skew = jnp.tile(line, (1, BLOCK))[:, :BLOCK * (n - 1)].reshape(WIN_HEADS, BLOCK, n - 1)
    bias = skew[:, :, BLOCK - 1:BLOCK - 1 + WIN_SPAN]
    bias_t = jnp.transpose(bias, (0, 2, 1))
    key = np.arange(WIN_SPAN)[None, :, None]
    first = jnp.where(key < WINDOW, NEG_INF, bias_t)
    last = jnp.where(key >= WINDOW + BLOCK, NEG_INF, bias_t)
    return jnp.stack([first, bias_t, last])


def _win_kernel(sink_ref, q_ref, kp_ref, kc_ref, kn_ref, vp_ref, vc_ref, vn_ref, bias_ref, o_ref,
                k_scr, vt_scr, *, nblk):
    i = pl.program_id(1)
    k_scr[0] = kp_ref[0]
    vt_scr[0] = vp_ref[0]
    for b in range(WIN_QB):
        k_scr[1 + b] = kc_ref[0, b * BLOCK:(b + 1) * BLOCK, :]
        vt_scr[1 + b] = vc_ref[0, :, b * BLOCK:(b + 1) * BLOCK]
    k_scr[WIN_QB + 1] = kn_ref[0]
    vt_scr[WIN_QB + 1] = vn_ref[0]

    def block(jb, carry):
        g = i * WIN_QB + jb
        variant = jnp.where(g == 0, 0, jnp.where(g == nblk - 1, 2, 1))
        rows = pl.ds(pl.multiple_of(jb * BLOCK, BLOCK), BLOCK)
        k = jnp.concatenate([k_scr[jb], k_scr[jb + 1], k_scr[jb + 2]], axis=0)
        vt = jnp.concatenate([vt_scr[jb], vt_scr[jb + 1], vt_scr[jb + 2]], axis=1)
        low = lax.broadcasted_iota(jnp.int32, k.shape, 1) < HEAD_DIM
        k_half = (jnp.where(low, k, 0), jnp.where(low, 0, k))
        ch = lax.broadcasted_iota(jnp.int32, vt.shape, 0)
        den_row = (HEAD_DIM, 0)
        vt_half = (jnp.where(ch < HEAD_DIM, vt, jnp.where(ch == den_row[0], 1, 0).astype(BF16)),
                   jnp.where(ch >= HEAD_DIM, vt, jnp.where(ch == den_row[1], 1, 0).astype(BF16)))
        for t in range(WIN_PAIRS):
            qp = q_ref[0, rows, t * 128:(t + 1) * 128]
            halves = []
            for j in range(WIN_KV_HEADS):
                h = t + j * WIN_GROUP
                sink = sink_ref[h]
                s = _dot_nt(k_half[j], qp) + bias_ref[variant, h]
                m = jnp.maximum(jnp.max(s, axis=0, keepdims=True), sink)
                p = jnp.exp(s - m).astype(BF16)
                o = _dot(vt_half[j], p)
                den = o[den_row[j]:den_row[j] + 1] + jnp.exp(sink - m)
                halves.append(o[j * HEAD_DIM:(j + 1) * HEAD_DIM] * (1.0 / den))
            o_ref[0, rows, t * 128:(t + 1) * 128] = jnp.concatenate(halves, axis=0).T.astype(BF16)
        return carry

    lax.fori_loop(0, WIN_QB, block, 0, unroll=True)


def _win_attn(q, k, vt, sink, bias):
    bsz, seq, _ = q.shape
    nblk = seq // BLOCK
    assert nblk >= 2 and nblk % WIN_QB == 0

    def edge(shift):
        return lambda b, i: jnp.clip(i * WIN_QB + shift, 0, nblk - 1)

    def k_edge(shift):
        return pl.BlockSpec((1, BLOCK, WIN_KV), lambda b, i: (b, edge(shift)(b, i), 0))

    def v_edge(shift):
        return pl.BlockSpec((1, WIN_KV, BLOCK), lambda b, i: (b, 0, edge(shift)(b, i)))

    qspec = pl.BlockSpec((1, WIN_QB * BLOCK, WIN_Q), lambda b, i: (b, i, 0))
    return pl.pallas_call(
        functools.partial(_win_kernel, nblk=nblk),
        grid=(bsz, nblk // WIN_QB),
        in_specs=[pl.BlockSpec(memory_space=pltpu.SMEM), qspec,
                  k_edge(-1), pl.BlockSpec((1, WIN_QB * BLOCK, WIN_KV), lambda b, i: (b, i, 0)), k_edge(WIN_QB),
                  v_edge(-1), pl.BlockSpec((1, WIN_KV, WIN_QB * BLOCK), lambda b, i: (b, 0, i)), v_edge(WIN_QB),
                  _resident(bias.shape)],
        out_specs=qspec,
        out_shape=jax.ShapeDtypeStruct((bsz, seq, WIN_Q), BF16),
        scratch_shapes=[pltpu.VMEM((WIN_QB + 2, BLOCK, WIN_KV), BF16),
                        pltpu.VMEM((WIN_QB + 2, WIN_KV, BLOCK), BF16)],
        compiler_params=_params("parallel", "parallel"),
        name="win_attn",
    )(sink, q, k, k, k, vt, vt, vt, bias)


def _na_bias(rpb):
    c = np.arange(GRID_W)
    col_start = np.clip(c - NA_COLS // 2, 0, GRID_W - NA_COLS)
    col = np.arange(GRID_W)
    inside = (col[None, :] >= col_start[:, None]) & (col[None, :] < col_start[:, None] + NA_COLS)
    b = rpb.astype(F32)
    n = 2 * GRID_W
    lead = GRID_W - NA_COLS
    line = jnp.pad(b, ((0, 0),) * 2 + ((lead, n - lead - (2 * NA_COLS - 1)),))
    skew = jnp.tile(line, (1, 1, GRID_W))[..., :GRID_W * (n - 1)]
    skew = skew.reshape(b.shape[:2] + (GRID_W, n - 1))[..., GRID_W - 1:2 * GRID_W - 1]
    b = jnp.where(inside[None, None], skew, NEG_INF)
    b = b.reshape((NA_HEADS // 2, 2) + b.shape[1:])
    b = jnp.transpose(b, (0, 2, 4, 1, 3))
    return b.reshape(NA_HEADS // 2, (2 * NA_ROWS - 1) * GRID_W, 2 * GRID_W)


def _na_kernel(q_ref, k_ref, v_ref, bias_ref, o_ref, *, rows):
    j = pl.program_id(1)
    win = NA_ROWS * GRID_W
    low = lax.broadcasted_iota(jnp.int32, (GRID_W, 2 * HEAD_DIM), 1) < HEAD_DIM

    def row(t, carry):
        r = j * NA_Q_ROWS + t
        rs = jnp.clip(r - NA_ROWS // 2, 0, rows - NA_ROWS)
        keys = pl.ds(pl.multiple_of(rs * GRID_W, GRID_W), win)
        qrows = pl.ds(pl.multiple_of(t * GRID_W, GRID_W), GRID_W)
        brows = pl.ds(pl.multiple_of((NA_ROWS - 1 - (r - rs)) * GRID_W, GRID_W), win)
        for tile in range(NA_HEADS // 2):
            lanes = slice(tile * 2 * HEAD_DIM, (tile + 1) * 2 * HEAD_DIM)
            q = q_ref[0, qrows, lanes]
            q_bd = jnp.concatenate([jnp.where(low, q, 0), jnp.where(low, 0, q)], axis=0)
            s = _dot_nt(k_ref[0, keys, lanes], q_bd) + bias_ref[tile, brows, :]
            m = jnp.max(s, axis=0, keepdims=True)
            p = jnp.exp(s - m)
            den = jnp.sum(p, axis=0, keepdims=True)
            o = lax.dot_general(v_ref[0, keys, lanes], p.astype(BF16), (((0,), (0,)), ((), ())),
                                preferred_element_type=F32)
            ot = (o * (1.0 / den)).T
            o_ref[0, qrows, lanes] = jnp.where(low, ot[:GRID_W], ot[GRID_W:]).astype(BF16)
        return carry

    lax.fori_loop(0, NA_Q_ROWS, row, 0, unroll=True)


def _na_attn(q, k, v, bias):
    bsz, seq, _ = q.shape
    rows = seq // GRID_W
    qspec = pl.BlockSpec((1, NA_Q_ROWS * GRID_W, NA_W), lambda b, j: (b, j, 0))
    kvspec = pl.BlockSpec((1, seq, NA_W), lambda b, j: (b, 0, 0))
    return pl.pallas_call(
        functools.partial(_na_kernel, rows=rows),
        grid=(bsz, rows // NA_Q_ROWS),
        in_specs=[qspec, kvspec, kvspec, _resident(bias.shape)],
        out_specs=qspec,
        out_shape=jax.ShapeDtypeStruct((bsz, seq, NA_W), BF16),
        compiler_params=_params("parallel", "arbitrary"),
        name="na_attn",
    )(q, k, v, bias)


def _mixout_kernel(x_ref, g_ref, za_ref, ob_ref, oc_ref, wga_ref, wgb_ref, wgc_ref,
                   wa_ref, wb_ref, wc_ref, wo_ref, o_ref):
    x = x_ref[...]
    h = _rms(x, g_ref[...]).astype(BF16)
    y = jax.nn.sigmoid(_dot(h, wga_ref[...])) * _dot(za_ref[...], wa_ref[...])
    y = y + jax.nn.sigmoid(_dot(h, wgb_ref[...])) * _dot(ob_ref[...], wb_ref[...])
    y = y + jax.nn.sigmoid(_dot(h, wgc_ref[...])) * _dot(oc_ref[...], wc_ref[...])
    o_ref[...] = x + _dot(y.astype(BF16), wo_ref[...])


def _mixout(x, g, za, ob, oc, w_in, wa, wb, wc, wo):
    t, d = x.shape
    assert OFF_GATE % d == 0

    def tile(n):
        return pl.BlockSpec((TOKEN_TILE, n), lambda i: (i, 0))

    def gate_cols(c):
        blk = OFF_GATE // d + c
        return pl.BlockSpec((d, d), lambda i: (0, blk), pipeline_mode=pl.Buffered(1))

    return pl.pallas_call(
        _mixout_kernel,
        grid=(t // TOKEN_TILE,),
        in_specs=[tile(d), _resident((1, d)), tile(CONV_CH), tile(WIN_Q), tile(NA_W),
                  gate_cols(0), gate_cols(1), gate_cols(2),
                  _resident(wa.shape), _resident(wb.shape), _resident(wc.shape), _resident(wo.shape)],
        out_specs=tile(d),
        out_shape=jax.ShapeDtypeStruct((t, d), F32),
        compiler_params=_params("parallel"),
        name="mixout",
    )(x, g, za, ob, oc, w_in, w_in, w_in, wa, wb, wc, wo)


def _memkv_kernel(m_ref, g_ref, w_ref, k_ref, v_ref):
    h = _rms(m_ref[0], g_ref[...]).astype(BF16)
    k_ref[0] = _dot(h, w_ref[:, :D_MODEL]).astype(BF16)
    v_ref[0] = _dot(h, w_ref[:, D_MODEL:]).astype(BF16)


def _memkv(mem, g, w):
    bsz, mlen, d = mem.shape
    blk = pl.BlockSpec((1, mlen, d), lambda b: (b, 0, 0))
    return pl.pallas_call(
        _memkv_kernel,
        grid=(bsz,),
        in_specs=[blk, _resident((1, d)), _resident(w.shape)],
        out_specs=[blk, blk],
        out_shape=[jax.ShapeDtypeStruct((bsz, mlen, d), BF16)] * 2,
        compiler_params=_params("parallel"),
        name="memkv",
    )(mem, g, w)


def _cross_kernel(x_ref, g_ref, k_ref, v_ref, wq_ref, wo_ref, o_ref):
    x = x_ref[0]
    h = _rms(x, g_ref[...]).astype(BF16)
    scale = X_HEAD_DIM ** -0.5
    q = (_dot(h, wq_ref[...]) * scale).astype(BF16)
    k = k_ref[0]
    v = v_ref[0]
    outs = []
    for hd in range(X_HEADS):
        hs = slice(hd * X_HEAD_DIM, (hd + 1) * X_HEAD_DIM)
        s = _dot_nt(q[:, hs], k[:, hs])
        m = jnp.max(s, axis=-1, keepdims=True)
        p = jnp.exp(s - m)
        den = jnp.sum(p, axis=-1, keepdims=True)
        outs.append((_dot(p.astype(BF16), v[:, hs]) / den).astype(BF16))
    o = jnp.concatenate(outs, axis=-1)
    o_ref[0] = x + _dot(o, wo_ref[...])


def _cross(x, g, k, v, wq, wo):
    bsz, seq, d = x.shape
    mlen = k.shape[1]
    tile = pl.BlockSpec((1, TOKEN_TILE, d), lambda b, i: (b, i, 0))
    kv = pl.BlockSpec((1, mlen, d), lambda b, i: (b, 0, 0))
    return pl.pallas_call(
        _cross_kernel,
        grid=(bsz, seq // TOKEN_TILE),
        in_specs=[tile, _resident((1, d)), kv, kv, _resident(wq.shape), _resident(wo.shape)],
        out_specs=tile,
        out_shape=jax.ShapeDtypeStruct((bsz, seq, d), F32),
        compiler_params=_params("parallel", "arbitrary"),
        name="cross",
    )(x, g, k, v, wq, wo)


def kernel(x, mem, norm_ffn1, ffn1_w_gate, ffn1_w_up, ffn1_w_down, norm_mix, w_in, conv_dw_w, conv_dw_b, conv_ln_g, conv_ln_b, conv_w_out, win_sink, t5_bias, win_w_out, na_rpb, na_w_out, w_out, norm_cross, norm_mem, cross_w_q, cross_w_kv, cross_w_o, norm_ffn2, ffn2_w_gate, ffn2_w_up, ffn2_w_down, norm_final):
    bsz, seq, d = x.shape
    depth = w_in.shape[0]
    t = bsz * seq
    assert d == D_MODEL and t % TOKEN_TILE == 0 and seq % TOKEN_TILE == 0
    assert seq % (NA_Q_ROWS * GRID_W) == 0 and seq // GRID_W >= NA_ROWS and seq % CONV_ROWS == 0

    def row(v):
        return v.reshape(1, -1).astype(F32)

    def bf(w):
        return w.astype(BF16)

    win_bias = _win_bias(t5_bias)
    win_heads = [h for tl in range(WIN_PAIRS) for h in (tl, tl + WIN_GROUP)]
    gf = row(norm_final)
    xt = x.reshape(t, d)
    for l in range(depth):
        xt = _ffn(xt, row(norm_ffn1[l]), bf(ffn1_w_gate[l]), bf(ffn1_w_up[l]), bf(ffn1_w_down[l]),
                  gf, False)
        w_in_l = bf(w_in[l])
        w_bq_l = jnp.concatenate(
            [w_in_l[:, OFF_BQ + h * HEAD_DIM:OFF_BQ + (h + 1) * HEAD_DIM] for h in win_heads], axis=1)
        win_w_out_l = jnp.concatenate(
            [bf(win_w_out[l][h * HEAD_DIM:(h + 1) * HEAD_DIM]) for h in win_heads], axis=0)
        z, bq, bk, bv, cq, ck, cv = _inproj(xt, row(norm_mix[l]), w_in_l, w_bq_l)

        def seqs(a):
            return a.reshape(bsz, seq, a.shape[-1])

        za = _conv(seqs(z), conv_dw_w[l], row(conv_dw_b[l]), row(conv_ln_g[l]), row(conv_ln_b[l]))
        ob = _win_attn(seqs(bq), seqs(bk), jnp.swapaxes(seqs(bv), 1, 2), win_sink[l].astype(F32), win_bias)
        oc = _na_attn(seqs(cq), seqs(ck), seqs(cv), _na_bias(na_rpb[l]))
        xt = _mixout(xt, row(norm_mix[l]), za.reshape(t, -1), ob.reshape(t, -1), oc.reshape(t, -1), w_in_l,
                     bf(conv_w_out[l]), win_w_out_l, bf(na_w_out[l]), bf(w_out[l]))
        mk, mv = _memkv(mem, row(norm_mem[l]), bf(cross_w_kv[l]))
        xt = _cross(xt.reshape(bsz, seq, d), row(norm_cross[l]), mk, mv,
                    bf(cross_w_q[l]), bf(cross_w_o[l])).reshape(t, d)
        xt = _ffn(xt, row(norm_ffn2[l]), bf(ffn2_w_gate[l]), bf(ffn2_w_up[l]), bf(ffn2_w_down[l]),
                  gf, l == depth - 1)
    return xt.reshape(bsz, seq, d)
```

```python
import functools

import numpy as np
import jax
import jax.numpy as jnp
from jax import lax
from jax.experimental import pallas as pl
from jax.experimental.pallas import tpu as pltpu

F32 = jnp.float32
BF16 = jnp.bfloat16

D_MODEL = 1024
EPS = 1e-6
NEG_INF = -1e30
HEAD_DIM = 64
D_FF = 11 * D_MODEL // 4
CONV_CH = D_MODEL // 4
CONV_WIDTH = 31
WIN_HEADS = (D_MODEL // 2) // HEAD_DIM
WIN_KV_HEADS = 2
WIN_GROUP = WIN_HEADS // WIN_KV_HEADS
WINDOW = 128
BLOCK = 128
T5_BUCKETS = 32
T5_MAX_DIST = 128
NA_HEADS = (D_MODEL // 4) // HEAD_DIM
NA_ROWS = 8
NA_COLS = 16
GRID_W = 64
X_HEADS = 4
X_HEAD_DIM = D_MODEL // X_HEADS
N_BRANCH = 3

WIN_Q = WIN_HEADS * HEAD_DIM
WIN_KV = WIN_KV_HEADS * HEAD_DIM
NA_W = NA_HEADS * HEAD_DIM
OFF_CONV = 0
OFF_BQ = OFF_CONV + 2 * CONV_CH
OFF_BK = OFF_BQ + WIN_Q
OFF_BV = OFF_BK + WIN_KV
OFF_CQ = OFF_BV + WIN_KV
OFF_CK = OFF_CQ + NA_W
OFF_CV = OFF_CK + NA_W
OFF_GATE = OFF_CV + NA_W
IN_WIDTH = OFF_GATE + N_BRANCH * D_MODEL

VMEM_LIMIT_BYTES = 56 * 1024 * 1024
TOKEN_TILE = 1024
FF_CHUNK = 256
CONV_ROWS = 128
CONV_PAD = 16
NA_Q_ROWS = 16
NA_UNROLL = 8


def _params(*sem, flags=None):
    return pltpu.CompilerParams(dimension_semantics=sem, vmem_limit_bytes=VMEM_LIMIT_BYTES, flags=flags)


def _resident(shape):
    zeros = (0,) * len(shape)
    return pl.BlockSpec(shape, lambda *_: zeros, pipeline_mode=pl.Buffered(1))


def _layer(w, l, cols=None):
    shape = w.shape[1:] if cols is None else w.shape[1:-1] + (cols,)
    zeros = (0,) * len(shape)
    return pl.BlockSpec((None,) + shape, lambda *_: (l,) + zeros, pipeline_mode=pl.Buffered(1))


def _rms(x, g):
    ms = jnp.mean(x * x, axis=-1, keepdims=True)
    return x * lax.rsqrt(ms + EPS) * g


def _dot(a, b):
    return jnp.dot(a, b, preferred_element_type=F32)


def _dot_nt(a, b):
    return lax.dot_general(a, b, (((1,), (1,)), ((), ())), preferred_element_type=F32)


def _ffn_kernel(x_ref, g_ref, wg_ref, wu_ref, wd_ref, gf_ref, o_ref, act_ref, *, final_norm):
    x = x_ref[...]
    h = _rms(x, g_ref[...]).astype(BF16)
    for c in range(D_FF // FF_CHUNK):
        sl = slice(c * FF_CHUNK, (c + 1) * FF_CHUNK)
        a = _dot(h, wg_ref[:, sl])
        b = _dot(h, wu_ref[:, sl])
        act_ref[:, sl] = (a * jax.nn.sigmoid(a) * b).astype(BF16)
    y = x + 0.5 * _dot(act_ref[...], wd_ref[...])
    if final_norm:
        y = _rms(y, gf_ref[...])
    o_ref[...] = y


def _ffn(x, g, wg, wu, wd, gf, final_norm, l):
    t, d = x.shape
    tile = pl.BlockSpec((TOKEN_TILE, d), lambda i: (i, 0))
    return pl.pallas_call(
        functools.partial(_ffn_kernel, final_norm=final_norm),
        grid=(t // TOKEN_TILE,),
        in_specs=[tile, _resident((1, d)), _layer(wg, l), _layer(wu, l), _layer(wd, l), _resident((1, d))],
        out_specs=tile,
        out_shape=jax.ShapeDtypeStruct((t, d), F32),
        scratch_shapes=[pltpu.VMEM((TOKEN_TILE, D_FF), BF16)],
        compiler_params=_params("parallel"),
        name="ffn",
    )(x, g, wg, wu, wd, gf)


def _inproj_kernel(x_ref, g_ref, w_ref, wq_ref, z_ref, bq_ref, bk_ref, bv_ref, cq_ref, ck_ref, cv_ref):
    h = _rms(x_ref[...], g_ref[...]).astype(BF16)
    scale = HEAD_DIM ** -0.5
    ua = _dot(h, w_ref[:, OFF_CONV:OFF_BQ])
    z_ref[...] = ua[:, :CONV_CH] * jax.nn.sigmoid(ua[:, CONV_CH:])
    bq_ref[...] = (_dot(h, wq_ref[...]) * scale).astype(BF16)
    bkv = _dot(h, w_ref[:, OFF_BK:OFF_CQ])
    bk_ref[...] = bkv[:, :WIN_KV].astype(BF16)
    bv_ref[...] = bkv[:, WIN_KV:].astype(BF16)
    cq_ref[...] = (_dot(h, w_ref[:, OFF_CQ:OFF_CK]) * scale).astype(BF16)
    ck_ref[...] = _dot(h, w_ref[:, OFF_CK:OFF_CV]).astype(BF16)
    cv_ref[...] = _dot(h, w_ref[:, OFF_CV:OFF_GATE]).astype(BF16)


def _inproj(x, g, w, wq, l):
    t, d = x.shape
    widths = (CONV_CH, WIN_Q, WIN_KV, WIN_KV, NA_W, NA_W, NA_W)
    dtypes = (F32,) + (BF16,) * 6

    def tile(n):
        return pl.BlockSpec((TOKEN_TILE, n), lambda i: (i, 0))

    return pl.pallas_call(
        _inproj_kernel,
        grid=(t // TOKEN_TILE,),
        in_specs=[tile(d), _resident((1, d)), _layer(w, l, cols=OFF_GATE), _layer(wq, l)],
        out_specs=[tile(n) for n in widths],
        out_shape=[jax.ShapeDtypeStruct((t, n), dt) for n, dt in zip(widths, dtypes)],
        compiler_params=_params("parallel"),
        name="inproj",
    )(x, g, w, wq)


def _conv_kernel(z_ref, w_ref, b_ref, lg_ref, lb_ref, o_ref, zp_ref, *, seq):
    zeros = jnp.zeros((CONV_PAD, CONV_CH), F32)
    zp_ref[0:CONV_PAD, :] = zeros
    zp_ref[CONV_PAD + seq:CONV_PAD + seq + CONV_PAD, :] = zeros
    zp_ref[CONV_PAD:CONV_PAD + seq, :] = z_ref[0]
    w = w_ref[...]
    half = CONV_WIDTH // 2
    lead = CONV_PAD - half
    n_shift = 8
    slab_rows = CONV_ROWS + 2 * CONV_PAD
    span = slab_rows - n_shift

    def chunk(c, carry):
        base = pl.multiple_of(c * CONV_ROWS, CONV_ROWS)
        slab = zp_ref[pl.ds(base, slab_rows), :]
        acc = jnp.zeros((CONV_ROWS, CONV_CH), F32) + b_ref[...]
        for s in range(n_shift):
            shifted = slab if s == 0 else pltpu.roll(slab, slab_rows - s, axis=0)
            for a in range(0, span - CONV_ROWS + 1, n_shift):
                k = a + s - lead
                if 0 <= k < CONV_WIDTH:
                    acc = acc + shifted[a:a + CONV_ROWS, :] * w[k:k + 1, :]
        mu = jnp.mean(acc, axis=-1, keepdims=True)
        cen = acc - mu
        var = jnp.mean(cen * cen, axis=-1, keepdims=True)
        y = cen * lax.rsqrt(var + EPS) * lg_ref[...] + lb_ref[...]
        o_ref[0, pl.ds(base, CONV_ROWS), :] = (y * jax.nn.sigmoid(y)).astype(BF16)
        return carry

    lax.fori_loop(0, seq // CONV_ROWS, chunk, 0, unroll=4)


def _conv(z, w, b, lg, lb):
    bsz, seq, ch = z.shape
    blk = pl.BlockSpec((1, seq, ch), lambda i: (i, 0, 0))
    return pl.pallas_call(
        functools.partial(_conv_kernel, seq=seq),
        grid=(bsz,),
        in_specs=[blk, _resident((CONV_WIDTH, ch)), _resident((1, ch)), _resident((1, ch)),
                  _resident((1, ch))],
        out_specs=blk,
        out_shape=jax.ShapeDtypeStruct((bsz, seq, ch), BF16),
        scratch_shapes=[pltpu.VMEM((seq + 2 * CONV_PAD, ch), F32)],
        compiler_params=_params("parallel"),
        name="conv",
    )(z, w, b, lg, lb)


def _t5_buckets(rel):
    half = T5_BUCKETS // 2
    max_exact = half // 2
    ret = (rel > 0).astype(np.int32) * half
    n = np.abs(rel)
    large = max_exact + (np.log(np.maximum(n, 1) / max_exact)
                         / np.log(T5_MAX_DIST / max_exact) * (half - max_exact)).astype(np.int32)
    large = np.minimum(large, half - 1)
    return ret + np.where(n < max_exact, n, large)


WIN_PAIRS = WIN_HEADS // WIN_KV_HEADS
WIN_SPAN = BLOCK + 2 * WINDOW
WIN_QB = 8
WIN_UNROLL = 4
assert WIN_KV_HEADS == 2 and 2 * HEAD_DIM == 128


def _win_head_perm():
    heads = [h for t in range(WIN_PAIRS) for h in (t, t + WIN_GROUP)]
    return np.concatenate([np.arange(h * HEAD_DIM, (h + 1) * HEAD_DIM) for h in heads])


def _win_bias(t5_table):
    n = WIN_SPAN + BLOCK
    rel = np.arange(n) - (n // 2 - 1)
    line = jnp.where((np.abs(rel) <= WINDOW)[:, None], t5_table[_t5_buckets(rel)].astype(F32), NEG_INF)
    line = jnp.transpose(line)
    skew = jnp.tile(line, (1, BLOCK))[:, :BLOCK * (n - 1)].reshape(WIN_HEADS, BLOCK, n - 1)
    bias = skew[:, :, BLOCK - 1:BLOCK - 1 + WIN_SPAN]
    bias_t = jnp.transpose(bias, (0, 2, 1))
    key = np.arange(WIN_SPAN)[None, :, None]
    first = jnp.where(key < WINDOW, NEG_INF, bias_t)
    last = jnp.where(key >= WINDOW + BLOCK, NEG_INF, bias_t)
    return jnp.stack([first, bias_t, last])


def _win_kernel(sink_ref, q_ref, kp_ref, kc_ref, kn_ref, vp_ref, vc_ref, vn_ref, bias_ref, o_ref,
                k_scr, vt_scr, *, nblk):
    i = pl.program_id(1)
    k_scr[0] = kp_ref[0]
    vt_scr[0] = vp_ref[0]
    for b in range(WIN_QB):
        k_scr[1 + b] = kc_ref[0, b * BLOCK:(b + 1) * BLOCK, :]
        vt_scr[1 + b] = vc_ref[0, :, b * BLOCK:(b + 1) * BLOCK]
    k_scr[WIN_QB + 1] = kn_ref[0]
    vt_scr[WIN_QB + 1] = vn_ref[0]

    def block(jb, carry):
        g = i * WIN_QB + jb
        variant = jnp.where(g == 0, 0, jnp.where(g == nblk - 1, 2, 1))
        rows = pl.ds(pl.multiple_of(jb * BLOCK, BLOCK), BLOCK)
        k = jnp.concatenate([k_scr[jb], k_scr[jb + 1], k_scr[jb + 2]], axis=0)
        vt = jnp.concatenate([vt_scr[jb], vt_scr[jb + 1], vt_scr[jb + 2]], axis=1)
        low = lax.broadcasted_iota(jnp.int32, k.shape, 1) < HEAD_DIM
        k_half = (jnp.where(low, k, 0), jnp.where(low, 0, k))
        ch = lax.broadcasted_iota(jnp.int32, vt.shape, 0)
        den_row = (HEAD_DIM, 0)
        vt_half = (jnp.where(ch < HEAD_DIM, vt, jnp.where(ch == den_row[0], 1, 0).astype(BF16)),
                   jnp.where(ch >= HEAD_DIM, vt, jnp.where(ch == den_row[1], 1, 0).astype(BF16)))
        for t in range(WIN_PAIRS):
            qp = q_ref[0, rows, t * 128:(t + 1) * 128]
            halves = []
            for j in range(WIN_KV_HEADS):
                h = t + j * WIN_GROUP
                sink = sink_ref[h]
                s = _dot_nt(k_half[j], qp) + bias_ref[variant, h]
                m = jnp.maximum(jnp.max(s, axis=0, keepdims=True), sink)
                p = jnp.exp(s - m).astype(BF16)
                o = _dot(vt_half[j], p)
                den = o[den_row[j]:den_row[j] + 1] + jnp.exp(sink - m)
                halves.append(o[j * HEAD_DIM:(j + 1) * HEAD_DIM] * (1.0 / den))
            o_ref[0, rows, t * 128:(t + 1) * 128] = jnp.concatenate(halves, axis=0).T.astype(BF16)
        return carry

    lax.fori_loop(0, WIN_QB, block, 0, unroll=WIN_UNROLL)


def _win_attn(q, k, vt, sink, bias):
    bsz, seq, _ = q.shape
    nblk = seq // BLOCK
    assert nblk >= 2 and nblk % WIN_QB == 0

    def edge(shift):
        return lambda b, i: jnp.clip(i * WIN_QB + shift, 0, nblk - 1)

    def k_edge(shift):
        return pl.BlockSpec((1, BLOCK, WIN_KV), lambda b, i: (b, edge(shift)(b, i), 0))

    def v_edge(shift):
        return pl.BlockSpec((1, WIN_KV, BLOCK), lambda b, i: (b, 0, edge(shift)(b, i)))

    qspec = pl.BlockSpec((1, WIN_QB * BLOCK, WIN_Q), lambda b, i: (b, i, 0))
    return pl.pallas_call(
        functools.partial(_win_kernel, nblk=nblk),
        grid=(bsz, nblk // WIN_QB),
        in_specs=[pl.BlockSpec(memory_space=pltpu.SMEM), qspec,
                  k_edge(-1), pl.BlockSpec((1, WIN_QB * BLOCK, WIN_KV), lambda b, i: (b, i, 0)), k_edge(WIN_QB),
                  v_edge(-1), pl.BlockSpec((1, WIN_KV, WIN_QB * BLOCK), lambda b, i: (b, 0, i)), v_edge(WIN_QB),
                  _resident(bias.shape)],
        out_specs=qspec,
        out_shape=jax.ShapeDtypeStruct((bsz, seq, WIN_Q), BF16),
        scratch_shapes=[pltpu.VMEM((WIN_QB + 2, BLOCK, WIN_KV), BF16),
                        pltpu.VMEM((WIN_QB + 2, WIN_KV, BLOCK), BF16)],
        compiler_params=_params("parallel", "parallel"),
        name="win_attn",
    )(sink, q, k, k, k, vt, vt, vt, bias)


def _na_bias(rpb):
    c = np.arange(GRID_W)
    col_start = np.clip(c - NA_COLS // 2, 0, GRID_W - NA_COLS)
    col = np.arange(GRID_W)
    inside = (col[None, :] >= col_start[:, None]) & (col[None, :] < col_start[:, None] + NA_COLS)
    b = rpb.astype(F32)
    n = 2 * GRID_W
    lead = GRID_W - NA_COLS
    line = jnp.pad(b, ((0, 0),) * 2 + ((lead, n - lead - (2 * NA_COLS - 1)),))
    skew = jnp.tile(line, (1, 1, GRID_W))[..., :GRID_W * (n - 1)]
    skew = skew.reshape(b.shape[:2] + (GRID_W, n - 1))[..., GRID_W - 1:2 * GRID_W - 1]
    b = jnp.where(inside[None, None], skew, NEG_INF)
    b = b.reshape((NA_HEADS // 2, 2) + b.shape[1:])
    b = jnp.transpose(b, (0, 2, 4, 1, 3))
    return b.reshape(NA_HEADS // 2, (2 * NA_ROWS - 1) * GRID_W, 2 * GRID_W)


def _na_kernel(q_ref, k_ref, v_ref, bias_ref, o_ref, *, rows):
    j = pl.program_id(1)
    win = NA_ROWS * GRID_W
    low = lax.broadcasted_iota(jnp.int32, (GRID_W, 2 * HEAD_DIM), 1) < HEAD_DIM

    def row(t, carry):
        r = j * NA_Q_ROWS + t
        rs = jnp.clip(r - NA_ROWS // 2, 0, rows - NA_ROWS)
        keys = pl.ds(pl.multiple_of(rs * GRID_W, GRID_W), win)
        qrows = pl.ds(pl.multiple_of(t * GRID_W, GRID_W), GRID_W)
        brows = pl.ds(pl.multiple_of((NA_ROWS - 1 - (r - rs)) * GRID_W, GRID_W), win)
        for tile in range(NA_HEADS // 2):
            lanes = slice(tile * 2 * HEAD_DIM, (tile + 1) * 2 * HEAD_DIM)
            q = q_ref[0, qrows, lanes]
            q_bd = jnp.concatenate([jnp.where(low, q, 0), jnp.where(low, 0, q)], axis=0)
            s = _dot_nt(k_ref[0, keys, lanes], q_bd) + bias_ref[tile, brows, :]
            m = jnp.max(s, axis=0, keepdims=True)
            p = jnp.exp(s - m)
            den = jnp.sum(p, axis=0, keepdims=True)
            o = lax.dot_general(v_ref[0, keys, lanes], p.astype(BF16), (((0,), (0,)), ((), ())),
                                preferred_element_type=F32)
            ot = (o * (1.0 / den)).T
            o_ref[0, qrows, lanes] = jnp.where(low, ot[:GRID_W], ot[GRID_W:]).astype(BF16)
        return carry

    lax.fori_loop(0, NA_Q_ROWS, row, 0, unroll=NA_UNROLL)


def _na_attn(q, k, v, bias):
    bsz, seq, _ = q.shape
    rows = seq // GRID_W
    qspec = pl.BlockSpec((1, NA_Q_ROWS * GRID_W, NA_W), lambda b, j: (b, j, 0))
    kvspec = pl.BlockSpec((1, seq, NA_W), lambda b, j: (b, 0, 0))
    return pl.pallas_call(
        functools.partial(_na_kernel, rows=rows),
        grid=(bsz, rows // NA_Q_ROWS),
        in_specs=[qspec, kvspec, kvspec, _resident(bias.shape)],
        out_specs=qspec,
        out_shape=jax.ShapeDtypeStruct((bsz, seq, NA_W), BF16),
        compiler_params=_params("parallel", "arbitrary"),
        name="na_attn",
    )(q, k, v, bias)


def _mixout_kernel(x_ref, g_ref, za_ref, ob_ref, oc_ref, wga_ref, wgb_ref, wgc_ref,
                   wa_ref, wb_ref, wc_ref, wo_ref, o_ref):
    x = x_ref[...]
    h = _rms(x, g_ref[...]).astype(BF16)
    y = jax.nn.sigmoid(_dot(h, wga_ref[...])) * _dot(za_ref[...], wa_ref[...])
    y = y + jax.nn.sigmoid(_dot(h, wgb_ref[...])) * _dot(ob_ref[...], wb_ref[...])
    y = y + jax.nn.sigmoid(_dot(h, wgc_ref[...])) * _dot(oc_ref[...], wc_ref[...])
    o_ref[...] = x + _dot(y.astype(BF16), wo_ref[...])


def _mixout(x, g, za, ob, oc, w_in, wa, wb, wc, wo, l):
    t, d = x.shape
    assert OFF_GATE % d == 0

    def tile(n):
        return pl.BlockSpec((TOKEN_TILE, n), lambda i: (i, 0))

    def gate_cols(c):
        blk = OFF_GATE // d + c
        return pl.BlockSpec((None, d, d), lambda i: (l, 0, blk), pipeline_mode=pl.Buffered(1))

    return pl.pallas_call(
        _mixout_kernel,
        grid=(t // TOKEN_TILE,),
        in_specs=[tile(d), _resident((1, d)), tile(CONV_CH), tile(WIN_Q), tile(NA_W),
                  gate_cols(0), gate_cols(1), gate_cols(2),
                  _layer(wa, l), _layer(wb, l), _layer(wc, l), _layer(wo, l)],
        out_specs=tile(d),
        out_shape=jax.ShapeDtypeStruct((t, d), F32),
        compiler_params=_params("parallel"),
        name="mixout",
    )(x, g, za, ob, oc, w_in, w_in, w_in, wa, wb, wc, wo)


def _memkv_kernel(m_ref, g_ref, w_ref, k_ref, v_ref):
    h = _rms(m_ref[0], g_ref[...]).astype(BF16)
    k_ref[0] = _dot(h, w_ref[:, :D_MODEL]).astype(BF16)
    v_ref[0] = _dot(h, w_ref[:, D_MODEL:]).astype(BF16)


def _memkv(mem, g, w, l):
    bsz, mlen, d = mem.shape
    blk = pl.BlockSpec((1, mlen, d), lambda b: (b, 0, 0))
    return pl.pallas_call(
        _memkv_kernel,
        grid=(bsz,),
        in_specs=[blk, _resident((1, d)), _layer(w, l)],
        out_specs=[blk, blk],
        out_shape=[jax.ShapeDtypeStruct((bsz, mlen, d), BF16)] * 2,
        compiler_params=_params("parallel"),
        name="memkv",
    )(mem, g, w)


def _cross_kernel(x_ref, g_ref, k_ref, v_ref, wq_ref, wo_ref, o_ref):
    x = x_ref[0]
    h = _rms(x, g_ref[...]).astype(BF16)
    scale = X_HEAD_DIM ** -0.5
    q = (_dot(h, wq_ref[...]) * scale).astype(BF16)
    k = k_ref[0]
    v = v_ref[0]
    outs = []
    for hd in range(X_HEADS):
        hs = slice(hd * X_HEAD_DIM, (hd + 1) * X_HEAD_DIM)
        s = _dot_nt(q[:, hs], k[:, hs])
        m = jnp.max(s, axis=-1, keepdims=True)
        p = jnp.exp(s - m)
        den = jnp.sum(p, axis=-1, keepdims=True)
        outs.append((_dot(p.astype(BF16), v[:, hs]) / den).astype(BF16))
    o = jnp.concatenate(outs, axis=-1)
    o_ref[0] = x + _dot(o, wo_ref[...])


def _cross(x, g, k, v, wq, wo, l):
    bsz, seq, d = x.shape
    mlen = k.shape[1]
    tile = pl.BlockSpec((1, TOKEN_TILE, d), lambda b, i: (b, i, 0))
    kv = pl.BlockSpec((1, mlen, d), lambda b, i: (b, 0, 0))
    return pl.pallas_call(
        _cross_kernel,
        grid=(bsz, seq // TOKEN_TILE),
        in_specs=[tile, _resident((1, d)), kv, kv, _layer(wq, l), _layer(wo, l)],
        out_specs=tile,
        out_shape=jax.ShapeDtypeStruct((bsz, seq, d), F32),
        compiler_params=_params("parallel", "arbitrary"),
        name="cross",
    )(x, g, k, v, wq, wo)


def kernel(x, mem, norm_ffn1, ffn1_w_gate, ffn1_w_up, ffn1_w_down, norm_mix, w_in, conv_dw_w, conv_dw_b, conv_ln_g, conv_ln_b, conv_w_out, win_sink, t5_bias, win_w_out, na_rpb, na_w_out, w_out, norm_cross, norm_mem, cross_w_q, cross_w_kv, cross_w_o, norm_ffn2, ffn2_w_gate, ffn2_w_up, ffn2_w_down, norm_final):
    bsz, seq, d = x.shape
    depth = w_in.shape[0]
    t = bsz * seq
    assert d == D_MODEL and t % TOKEN_TILE == 0 and seq % TOKEN_TILE == 0
    assert seq % (NA_Q_ROWS * GRID_W) == 0 and seq // GRID_W >= NA_ROWS and seq % CONV_ROWS == 0

    def row(v):
        return v.reshape(1, -1).astype(F32)

    def bf(w):
        return w.astype(BF16)

    win_bias = _win_bias(t5_bias)
    win_heads = [h for tl in range(WIN_PAIRS) for h in (tl, tl + WIN_GROUP)]
    gf = row(norm_final)
    ffn1 = (bf(ffn1_w_gate), bf(ffn1_w_up), bf(ffn1_w_down))
    ffn2 = (bf(ffn2_w_gate), bf(ffn2_w_up), bf(ffn2_w_down))
    w_in_b = bf(w_in)
    w_bq = jnp.concatenate(
        [w_in_b[:, :, OFF_BQ + h * HEAD_DIM:OFF_BQ + (h + 1) * HEAD_DIM] for h in win_heads], axis=2)
    win_w_out_b = jnp.concatenate(
        [bf(win_w_out[:, h * HEAD_DIM:(h + 1) * HEAD_DIM]) for h in win_heads], axis=1)
    mix_w = (bf(conv_w_out), win_w_out_b, bf(na_w_out), bf(w_out))
    cross_w = (bf(cross_w_q), bf(cross_w_o))
    cross_w_kv_b = bf(cross_w_kv)
    xt = x.reshape(t, d)
    for l in range(depth):
        xt = _ffn(xt, row(norm_ffn1[l]), *ffn1, gf, False, l)
        z, bq, bk, bv, cq, ck, cv = _inproj(xt, row(norm_mix[l]), w_in_b, w_bq, l)

        def seqs(a):
            return a.reshape(bsz, seq, a.shape[-1])

        za = _conv(seqs(z), conv_dw_w[l], row(conv_dw_b[l]), row(conv_ln_g[l]), row(conv_ln_b[l]))
        ob = _win_attn(seqs(bq), seqs(bk), jnp.swapaxes(seqs(bv), 1, 2), win_sink[l].astype(F32), win_bias)
        oc = _na_attn(seqs(cq), seqs(ck), seqs(cv), _na_bias(na_rpb[l]))
        xt = _mixout(xt, row(norm_mix[l]), za.reshape(t, -1), ob.reshape(t, -1), oc.reshape(t, -1), w_in_b,
                     *mix_w, l)
        mk, mv = _memkv(mem, row(norm_mem[l]), cross_w_kv_b, l)
        xt = _cross(xt.reshape(bsz, seq, d), row(norm_cross[l]), mk, mv, *cross_w, l).reshape(t, d)
        xt = _ffn(xt, row(norm_ffn2[l]), *ffn2, gf, l == depth - 1, l)
    return xt.reshape(bsz, seq, d)
```

```python
import functools

import numpy as np
import jax
import jax.numpy as jnp
from jax import lax
from jax.experimental import pallas as pl
from jax.experimental.pallas import tpu as pltpu

F32 = jnp.float32
BF16 = jnp.bfloat16

D_MODEL = 1024
EPS = 1e-6
LOG2E = 1.4426950408889634
NEG_INF = -1e30
HEAD_DIM = 64
D_FF = 11 * D_MODEL // 4
CONV_CH = D_MODEL // 4
CONV_WIDTH = 31
WIN_HEADS = (D_MODEL // 2) // HEAD_DIM
WIN_KV_HEADS = 2
WIN_GROUP = WIN_HEADS // WIN_KV_HEADS
WINDOW = 128
BLOCK = 128
T5_BUCKETS = 32
T5_MAX_DIST = 128
NA_HEADS = (D_MODEL // 4) // HEAD_DIM
NA_ROWS = 8
NA_COLS = 16
GRID_W = 64
X_HEADS = 4
X_HEAD_DIM = D_MODEL // X_HEADS
N_BRANCH = 3

WIN_Q = WIN_HEADS * HEAD_DIM
WIN_KV = WIN_KV_HEADS * HEAD_DIM
NA_W = NA_HEADS * HEAD_DIM
OFF_CONV = 0
OFF_BQ = OFF_CONV + 2 * CONV_CH
OFF_BK = OFF_BQ + WIN_Q
OFF_BV = OFF_BK + WIN_KV
OFF_CQ = OFF_BV + WIN_KV
OFF_CK = OFF_CQ + NA_W
OFF_CV = OFF_CK + NA_W
OFF_GATE = OFF_CV + NA_W
IN_WIDTH = OFF_GATE + N_BRANCH * D_MODEL

VMEM_LIMIT_BYTES = 56 * 1024 * 1024
TOKEN_TILE = 1024
FF_CHUNK = 256
MIX_CHUNK = 256
CONV_ROWS = 128
CONV_PAD = 16
NA_Q_ROWS = 16
NA_UNROLL = 8


def _params(*sem, flags=None):
    return pltpu.CompilerParams(dimension_semantics=sem, vmem_limit_bytes=VMEM_LIMIT_BYTES, flags=flags)


def _resident(shape):
    zeros = (0,) * len(shape)
    return pl.BlockSpec(shape, lambda *_: zeros, pipeline_mode=pl.Buffered(1))


def _layer(w, l, cols=None):
    shape = w.shape[1:] if cols is None else w.shape[1:-1] + (cols,)
    zeros = (0,) * len(shape)
    return pl.BlockSpec((None,) + shape, lambda *_: (l,) + zeros, pipeline_mode=pl.Buffered(1))


def _rms(x, g):
    ms = jnp.mean(x * x, axis=-1, keepdims=True)
    return x * lax.rsqrt(ms + EPS) * g


def _dot(a, b):
    return jnp.dot(a, b, preferred_element_type=F32)


def _dot_nt(a, b):
    return lax.dot_general(a, b, (((1,), (1,)), ((), ())), preferred_element_type=F32)


def _ffn_kernel(x_ref, g_ref, wg_ref, wu_ref, wd_ref, gf_ref, o_ref, act_ref, *, final_norm):
    x = x_ref[...]
    h = _rms(x, g_ref[...]).astype(BF16)
    for c in range(D_FF // FF_CHUNK):
        sl = slice(c * FF_CHUNK, (c + 1) * FF_CHUNK)
        a = _dot(h, wg_ref[:, sl])
        b = _dot(h, wu_ref[:, sl])
        act_ref[:, sl] = (a * jax.nn.sigmoid(a) * b).astype(BF16)
    y = x + 0.5 * _dot(act_ref[...], wd_ref[...])
    if final_norm:
        y = _rms(y, gf_ref[...])
    o_ref[...] = y


def _ffn(x, g, wg, wu, wd, gf, final_norm, l):
    t, d = x.shape
    tile = pl.BlockSpec((TOKEN_TILE, d), lambda i: (i, 0))
    return pl.pallas_call(
        functools.partial(_ffn_kernel, final_norm=final_norm),
        grid=(t // TOKEN_TILE,),
        in_specs=[tile, _resident((1, d)), _layer(wg, l), _layer(wu, l), _layer(wd, l), _resident((1, d))],
        out_specs=tile,
        out_shape=jax.ShapeDtypeStruct((t, d), F32),
        scratch_shapes=[pltpu.VMEM((TOKEN_TILE, D_FF), BF16)],
        compiler_params=_params("parallel"),
        name="ffn",
    )(x, g, wg, wu, wd, gf)


def _inproj_kernel(x_ref, g_ref, w_ref, wq_ref, z_ref, bq_ref, bk_ref, bv_ref, cq_ref, ck_ref, cv_ref):
    h = _rms(x_ref[...], g_ref[...]).astype(BF16)
    scale = HEAD_DIM ** -0.5 * LOG2E
    ua = _dot(h, w_ref[:, OFF_CONV:OFF_BQ])
    z_ref[...] = ua[:, :CONV_CH] * jax.nn.sigmoid(ua[:, CONV_CH:])
    bq_ref[...] = (_dot(h, wq_ref[...]) * scale).astype(BF16)
    bkv = _dot(h, w_ref[:, OFF_BK:OFF_CQ])
    bk_ref[...] = bkv[:, :WIN_KV].astype(BF16)
    bv_ref[...] = bkv[:, WIN_KV:].astype(BF16)
    cq_ref[...] = (_dot(h, w_ref[:, OFF_CQ:OFF_CK]) * scale).astype(BF16)
    ck_ref[...] = _dot(h, w_ref[:, OFF_CK:OFF_CV]).astype(BF16)
    cv_ref[...] = _dot(h, w_ref[:, OFF_CV:OFF_GATE]).astype(BF16)


def _inproj(x, g, w, wq, l):
    t, d = x.shape
    widths = (CONV_CH, WIN_Q, WIN_KV, WIN_KV, NA_W, NA_W, NA_W)
    dtypes = (F32,) + (BF16,) * 6

    def tile(n):
        return pl.BlockSpec((TOKEN_TILE, n), lambda i: (i, 0))

    return pl.pallas_call(
        _inproj_kernel,
        grid=(t // TOKEN_TILE,),
        in_specs=[tile(d), _resident((1, d)), _layer(w, l, cols=OFF_GATE), _layer(wq, l)],
        out_specs=[tile(n) for n in widths],
        out_shape=[jax.ShapeDtypeStruct((t, n), dt) for n, dt in zip(widths, dtypes)],
        compiler_params=_params("parallel"),
        name="inproj",
    )(x, g, w, wq)


def _conv_kernel(z_ref, w_ref, b_ref, lg_ref, lb_ref, o_ref, zp_ref, *, seq):
    zeros = jnp.zeros((CONV_PAD, CONV_CH), F32)
    zp_ref[0:CONV_PAD, :] = zeros
    zp_ref[CONV_PAD + seq:CONV_PAD + seq + CONV_PAD, :] = zeros
    zp_ref[CONV_PAD:CONV_PAD + seq, :] = z_ref[0]
    w = w_ref[...]
    half = CONV_WIDTH // 2
    lead = CONV_PAD - half
    n_shift = 8
    slab_rows = CONV_ROWS + 2 * CONV_PAD
    span = slab_rows - n_shift

    def chunk(c, carry):
        base = pl.multiple_of(c * CONV_ROWS, CONV_ROWS)
        slab = zp_ref[pl.ds(base, slab_rows), :]
        acc = jnp.zeros((CONV_ROWS, CONV_CH), F32) + b_ref[...]
        for s in range(n_shift):
            shifted = slab if s == 0 else pltpu.roll(slab, slab_rows - s, axis=0)
            for a in range(0, span - CONV_ROWS + 1, n_shift):
                k = a + s - lead
                if 0 <= k < CONV_WIDTH:
                    acc = acc + shifted[a:a + CONV_ROWS, :] * w[k:k + 1, :]
        mu = jnp.mean(acc, axis=-1, keepdims=True)
        cen = acc - mu
        var = jnp.mean(cen * cen, axis=-1, keepdims=True)
        y = cen * lax.rsqrt(var + EPS) * lg_ref[...] + lb_ref[...]
        o_ref[0, pl.ds(base, CONV_ROWS), :] = (y * jax.nn.sigmoid(y)).astype(BF16)
        return carry

    lax.fori_loop(0, seq // CONV_ROWS, chunk, 0, unroll=4)


def _conv(z, w, b, lg, lb):
    bsz, seq, ch = z.shape
    blk = pl.BlockSpec((1, seq, ch), lambda i: (i, 0, 0))
    return pl.pallas_call(
        functools.partial(_conv_kernel, seq=seq),
        grid=(bsz,),
        in_specs=[blk, _resident((CONV_WIDTH, ch)), _resident((1, ch)), _resident((1, ch)),
                  _resident((1, ch))],
        out_specs=blk,
        out_shape=jax.ShapeDtypeStruct((bsz, seq, ch), BF16),
        scratch_shapes=[pltpu.VMEM((seq + 2 * CONV_PAD, ch), F32)],
        compiler_params=_params("parallel"),
        name="conv",
    )(z, w, b, lg, lb)


def _t5_buckets(rel):
    half = T5_BUCKETS // 2
    max_exact = half // 2
    ret = (rel > 0).astype(np.int32) * half
    n = np.abs(rel)
    large = max_exact + (np.log(np.maximum(n, 1) / max_exact)
                         / np.log(T5_MAX_DIST / max_exact) * (half - max_exact)).astype(np.int32)
    large = np.minimum(large, half - 1)
    return ret + np.where(n < max_exact, n, large)


WIN_PAIRS = WIN_HEADS // WIN_KV_HEADS
WIN_SPAN = BLOCK + 2 * WINDOW
WIN_QB = 8
WIN_UNROLL = 4
assert WIN_KV_HEADS == 2 and 2 * HEAD_DIM == 128


def _win_head_perm():
    heads = [h for t in range(WIN_PAIRS) for h in (t, t + WIN_GROUP)]
    return np.concatenate([np.arange(h * HEAD_DIM, (h + 1) * HEAD_DIM) for h in heads])


def _win_bias(t5_table):
    n = WIN_SPAN + BLOCK
    rel = np.arange(n) - (n // 2 - 1)
    line = jnp.where((np.abs(rel) <= WINDOW)[:, None], t5_table[_t5_buckets(rel)].astype(F32), NEG_INF)
    line = jnp.transpose(line)
    skew = jnp.tile(line, (1, BLOCK))[:, :BLOCK * (n - 1)].reshape(WIN_HEADS, BLOCK, n - 1)
    bias = skew[:, :, BLOCK - 1:BLOCK - 1 + WIN_SPAN]
    bias_t = jnp.transpose(bias, (0, 2, 1))
    key = np.arange(WIN_SPAN)[None, :, None]
    first = jnp.where(key < WINDOW, NEG_INF, bias_t)
    last = jnp.where(key >= WINDOW + BLOCK, NEG_INF, bias_t)
    return jnp.stack([first, bias_t, last])


def _win_kernel(sink_ref, q_ref, kp_ref, kc_ref, kn_ref, vp_ref, vc_ref, vn_ref, bias_ref, o_ref,
                k_scr, vt_scr, *, nblk):
    i = pl.program_id(1)
    k_scr[0] = kp_ref[0]
    vt_scr[0] = vp_ref[0]
    for b in range(WIN_QB):
        k_scr[1 + b] = kc_ref[0, b * BLOCK:(b + 1) * BLOCK, :]
        vt_scr[1 + b] = vc_ref[0, :, b * BLOCK:(b + 1) * BLOCK]
    k_scr[WIN_QB + 1] = kn_ref[0]
    vt_scr[WIN_QB + 1] = vn_ref[0]

    def block(jb, carry):
        g = i * WIN_QB + jb
        variant = jnp.where(g == 0, 0, jnp.where(g == nblk - 1, 2, 1))
        rows = pl.ds(pl.multiple_of(jb * BLOCK, BLOCK), BLOCK)
        k = jnp.concatenate([k_scr[jb], k_scr[jb + 1], k_scr[jb + 2]], axis=0)
        vt = jnp.concatenate([vt_scr[jb], vt_scr[jb + 1], vt_scr[jb + 2]], axis=1)
        low = lax.broadcasted_iota(jnp.int32, k.shape, 1) < HEAD_DIM
        k_half = (jnp.where(low, k, 0), jnp.where(low, 0, k))
        ch = lax.broadcasted_iota(jnp.int32, vt.shape, 0)
        den_row = (HEAD_DIM, 0)
        vt_half = (jnp.where(ch < HEAD_DIM, vt, jnp.where(ch == den_row[0], 1, 0).astype(BF16)),
                   jnp.where(ch >= HEAD_DIM, vt, jnp.where(ch == den_row[1], 1, 0).astype(BF16)))
        for t in range(WIN_PAIRS):
            qp = q_ref[0, rows, t * 128:(t + 1) * 128]
            halves = []
            for j in range(WIN_KV_HEADS):
                h = t + j * WIN_GROUP
                sink = sink_ref[h]
                s = _dot_nt(k_half[j], qp) + bias_ref[variant, h]
                m = jnp.maximum(jnp.max(s, axis=0, keepdims=True), sink)
                p = jnp.exp2(s - m).astype(BF16)
                o = _dot(vt_half[j], p)
                den = o[den_row[j]:den_row[j] + 1] + jnp.exp2(sink - m)
                halves.append(o[j * HEAD_DIM:(j + 1) * HEAD_DIM] * (1.0 / den))
            o_ref[0, rows, t * 128:(t + 1) * 128] = jnp.concatenate(halves, axis=0).T.astype(BF16)
        return carry

    lax.fori_loop(0, WIN_QB, block, 0, unroll=WIN_UNROLL)


def _win_attn(q, k, vt, sink, bias):
    bsz, seq, _ = q.shape
    nblk = seq // BLOCK
    assert nblk >= 2 and nblk % WIN_QB == 0

    def edge(shift):
        return lambda b, i: jnp.clip(i * WIN_QB + shift, 0, nblk - 1)

    def k_edge(shift):
        return pl.BlockSpec((1, BLOCK, WIN_KV), lambda b, i: (b, edge(shift)(b, i), 0))

    def v_edge(shift):
        return pl.BlockSpec((1, WIN_KV, BLOCK), lambda b, i: (b, 0, edge(shift)(b, i)))

    qspec = pl.BlockSpec((1, WIN_QB * BLOCK, WIN_Q), lambda b, i: (b, i, 0))
    return pl.pallas_call(
        functools.partial(_win_kernel, nblk=nblk),
        grid=(bsz, nblk // WIN_QB),
        in_specs=[pl.BlockSpec(memory_space=pltpu.SMEM), qspec,
                  k_edge(-1), pl.BlockSpec((1, WIN_QB * BLOCK, WIN_KV), lambda b, i: (b, i, 0)), k_edge(WIN_QB),
                  v_edge(-1), pl.BlockSpec((1, WIN_KV, WIN_QB * BLOCK), lambda b, i: (b, 0, i)), v_edge(WIN_QB),
                  _resident(bias.shape)],
        out_specs=qspec,
        out_shape=jax.ShapeDtypeStruct((bsz, seq, WIN_Q), BF16),
        scratch_shapes=[pltpu.VMEM((WIN_QB + 2, BLOCK, WIN_KV), BF16),
                        pltpu.VMEM((WIN_QB + 2, WIN_KV, BLOCK), BF16)],
        compiler_params=_params("parallel", "parallel"),
        name="win_attn",
    )(sink, q, k, k, k, vt, vt, vt, bias)


def _na_bias(rpb):
    c = np.arange(GRID_W)
    col_start = np.clip(c - NA_COLS // 2, 0, GRID_W - NA_COLS)
    col = np.arange(GRID_W)
    inside = (col[None, :] >= col_start[:, None]) & (col[None, :] < col_start[:, None] + NA_COLS)
    b = rpb.astype(F32)
    n = 2 * GRID_W
    lead = GRID_W - NA_COLS
    line = jnp.pad(b, ((0, 0),) * 2 + ((lead, n - lead - (2 * NA_COLS - 1)),))
    skew = jnp.tile(line, (1, 1, GRID_W))[..., :GRID_W * (n - 1)]
    skew = skew.reshape(b.shape[:2] + (GRID_W, n - 1))[..., GRID_W - 1:2 * GRID_W - 1]
    b = jnp.where(inside[None, None], skew, NEG_INF)
    b = b.reshape((NA_HEADS // 2, 2) + b.shape[1:])
    b = jnp.transpose(b, (0, 2, 4, 1, 3))
    return b.reshape(NA_HEADS // 2, (2 * NA_ROWS - 1) * GRID_W, 2 * GRID_W)


def _na_kernel(q_ref, k_ref, v_ref, bias_ref, o_ref, *, rows):
    j = pl.program_id(1)
    win = NA_ROWS * GRID_W
    low = lax.broadcasted_iota(jnp.int32, (GRID_W, 2 * HEAD_DIM), 1) < HEAD_DIM

    def row(t, carry):
        r = j * NA_Q_ROWS + t
        rs = jnp.clip(r - NA_ROWS // 2, 0, rows - NA_ROWS)
        keys = pl.ds(pl.multiple_of(rs * GRID_W, GRID_W), win)
        qrows = pl.ds(pl.multiple_of(t * GRID_W, GRID_W), GRID_W)
        brows = pl.ds(pl.multiple_of((NA_ROWS - 1 - (r - rs)) * GRID_W, GRID_W), win)
        for tile in range(NA_HEADS // 2):
            lanes = slice(tile * 2 * HEAD_DIM, (tile + 1) * 2 * HEAD_DIM)
            q = q_ref[0, qrows, lanes]
            q_bd = jnp.concatenate([jnp.where(low, q, 0), jnp.where(low, 0, q)], axis=0)
            s = _dot_nt(k_ref[0, keys, lanes], q_bd) + bias_ref[tile, brows, :]
            m = jnp.max(s, axis=0, keepdims=True)
            p = jnp.exp2(s - m)
            den = jnp.sum(p, axis=0, keepdims=True)
            o = lax.dot_general(v_ref[0, keys, lanes], p.astype(BF16), (((0,), (0,)), ((), ())),
                                preferred_element_type=F32)
            ot = (o * (1.0 / den)).T
            o_ref[0, qrows, lanes] = jnp.where(low, ot[:GRID_W], ot[GRID_W:]).astype(BF16)
        return carry

    lax.fori_loop(0, NA_Q_ROWS, row, 0, unroll=NA_UNROLL)


def _na_attn(q, k, v, bias):
    bsz, seq, _ = q.shape
    rows = seq // GRID_W
    qspec = pl.BlockSpec((1, NA_Q_ROWS * GRID_W, NA_W), lambda b, j: (b, j, 0))
    kvspec = pl.BlockSpec((1, seq, NA_W), lambda b, j: (b, 0, 0))
    return pl.pallas_call(
        functools.partial(_na_kernel, rows=rows),
        grid=(bsz, rows // NA_Q_ROWS),
        in_specs=[qspec, kvspec, kvspec, _resident(bias.shape)],
        out_specs=qspec,
        out_shape=jax.ShapeDtypeStruct((bsz, seq, NA_W), BF16),
        compiler_params=_params("parallel", "arbitrary"),
        name="na_attn",
    )(q, k, v, bias)


def _mixout_kernel(x_ref, g_ref, za_ref, ob_ref, oc_ref, wga_ref, wgb_ref, wgc_ref,
                   wa_ref, wb_ref, wc_ref, wo_ref, o_ref, y_ref):
    x = x_ref[...]
    h = _rms(x, g_ref[...]).astype(BF16)
    branches = ((wga_ref, za_ref, wa_ref), (wgb_ref, ob_ref, wb_ref), (wgc_ref, oc_ref, wc_ref))
    for c in range(D_MODEL // MIX_CHUNK):
        sl = slice(c * MIX_CHUNK, (c + 1) * MIX_CHUNK)
        y = None
        for wg_ref, b_ref, w_ref in branches:
            term = jax.nn.sigmoid(_dot(h, wg_ref[:, sl])) * _dot(b_ref[...], w_ref[:, sl])
            y = term if y is None else y + term
        y_ref[:, sl] = y.astype(BF16)
    o_ref[...] = x + _dot(y_ref[...], wo_ref[...])


def _mixout(x, g, za, ob, oc, w_in, wa, wb, wc, wo, l):
    t, d = x.shape
    assert OFF_GATE % d == 0

    def tile(n):
        return pl.BlockSpec((TOKEN_TILE, n), lambda i: (i, 0))

    def gate_cols(c):
        blk = OFF_GATE // d + c
        return pl.BlockSpec((None, d, d), lambda i: (l, 0, blk), pipeline_mode=pl.Buffered(1))

    return pl.pallas_call(
        _mixout_kernel,
        grid=(t // TOKEN_TILE,),
        in_specs=[tile(d), _resident((1, d)), tile(CONV_CH), tile(WIN_Q), tile(NA_W),
                  gate_cols(0), gate_cols(1), gate_cols(2),
                  _layer(wa, l), _layer(wb, l), _layer(wc, l), _layer(wo, l)],
        out_specs=tile(d),
        out_shape=jax.ShapeDtypeStruct((t, d), F32),
        scratch_shapes=[pltpu.VMEM((TOKEN_TILE, d), BF16)],
        compiler_params=_params("parallel"),
        name="mixout",
    )(x, g, za, ob, oc, w_in, w_in, w_in, wa, wb, wc, wo)


def _memkv_kernel(m_ref, g_ref, w_ref, k_ref, v_ref):
    h = _rms(m_ref[0], g_ref[...]).astype(BF16)
    k_ref[0] = _dot(h, w_ref[:, :D_MODEL]).astype(BF16)
    v_ref[0] = _dot(h, w_ref[:, D_MODEL:]).astype(BF16)


def _memkv(mem, g, w, l):
    bsz, mlen, d = mem.shape
    blk = pl.BlockSpec((1, mlen, d), lambda b: (b, 0, 0))
    return pl.pallas_call(
        _memkv_kernel,
        grid=(bsz,),
        in_specs=[blk, _resident((1, d)), _layer(w, l)],
        out_specs=[blk, blk],
        out_shape=[jax.ShapeDtypeStruct((bsz, mlen, d), BF16)] * 2,
        compiler_params=_params("parallel"),
        name="memkv",
    )(mem, g, w)


def _cross_kernel(x_ref, g_ref, k_ref, v_ref, wq_ref, wo_ref, o_ref):
    x = x_ref[0]
    h = _rms(x, g_ref[...]).astype(BF16)
    scale = X_HEAD_DIM ** -0.5
    q = (_dot(h, wq_ref[...]) * scale).astype(BF16)
    k = k_ref[0]
    v = v_ref[0]
    outs = []
    for hd in range(X_HEADS):
        hs = slice(hd * X_HEAD_DIM, (hd + 1) * X_HEAD_DIM)
        s = _dot_nt(q[:, hs], k[:, hs])
        m = jnp.max(s, axis=-1, keepdims=True)
        p = jnp.exp(s - m)
        den = jnp.sum(p, axis=-1, keepdims=True)
        outs.append((_dot(p.astype(BF16), v[:, hs]) / den).astype(BF16))
    o = jnp.concatenate(outs, axis=-1)
    o_ref[0] = x + _dot(o, wo_ref[...])


def _cross(x, g, k, v, wq, wo, l):
    bsz, seq, d = x.shape
    mlen = k.shape[1]
    tile = pl.BlockSpec((1, TOKEN_TILE, d), lambda b, i: (b, i, 0))
    kv = pl.BlockSpec((1, mlen, d), lambda b, i: (b, 0, 0))
    return pl.pallas_call(
        _cross_kernel,
        grid=(bsz, seq // TOKEN_TILE),
        in_specs=[tile, _resident((1, d)), kv, kv, _layer(wq, l), _layer(wo, l)],
        out_specs=tile,
        out_shape=jax.ShapeDtypeStruct((bsz, seq, d), F32),
        compiler_params=_params("parallel", "arbitrary"),
        name="cross",
    )(x, g, k, v, wq, wo)


def kernel(x, mem, norm_ffn1, ffn1_w_gate, ffn1_w_up, ffn1_w_down, norm_mix, w_in, conv_dw_w, conv_dw_b, conv_ln_g, conv_ln_b, conv_w_out, win_sink, t5_bias, win_w_out, na_rpb, na_w_out, w_out, norm_cross, norm_mem, cross_w_q, cross_w_kv, cross_w_o, norm_ffn2, ffn2_w_gate, ffn2_w_up, ffn2_w_down, norm_final):
    bsz, seq, d = x.shape
    depth = w_in.shape[0]
    t = bsz * seq
    assert d == D_MODEL and t % TOKEN_TILE == 0 and seq % TOKEN_TILE == 0
    assert seq % (NA_Q_ROWS * GRID_W) == 0 and seq // GRID_W >= NA_ROWS and seq % CONV_ROWS == 0

    def row(v):
        return v.reshape(1, -1).astype(F32)

    def bf(w):
        return w.astype(BF16)

    win_bias = _win_bias(t5_bias * LOG2E)
    win_heads = [h for tl in range(WIN_PAIRS) for h in (tl, tl + WIN_GROUP)]
    gf = row(norm_final)
    ffn1 = (bf(ffn1_w_gate), bf(ffn1_w_up), bf(ffn1_w_down))
    ffn2 = (bf(ffn2_w_gate), bf(ffn2_w_up), bf(ffn2_w_down))
    w_in_b = bf(w_in)
    w_bq = jnp.concatenate(
        [w_in_b[:, :, OFF_BQ + h * HEAD_DIM:OFF_BQ + (h + 1) * HEAD_DIM] for h in win_heads], axis=2)
    win_w_out_b = jnp.concatenate(
        [bf(win_w_out[:, h * HEAD_DIM:(h + 1) * HEAD_DIM]) for h in win_heads], axis=1)
    mix_w = (bf(conv_w_out), win_w_out_b, bf(na_w_out), bf(w_out))
    cross_w = (bf(cross_w_q), bf(cross_w_o))
    cross_w_kv_b = bf(cross_w_kv)
    xt = x.reshape(t, d)
    for l in range(depth):
        xt = _ffn(xt, row(norm_ffn1[l]), *ffn1, gf, False, l)
        z, bq, bk, bv, cq, ck, cv = _inproj(xt, row(norm_mix[l]), w_in_b, w_bq, l)

        def seqs(a):
            return a.reshape(bsz, seq, a.shape[-1])

        za = _conv(seqs(z), conv_dw_w[l], row(conv_dw_b[l]), row(conv_ln_g[l]), row(conv_ln_b[l]))
        ob = _win_attn(seqs(bq), seqs(bk), jnp.swapaxes(seqs(bv), 1, 2), win_sink[l].astype(F32) * LOG2E, win_bias)
        oc = _na_attn(seqs(cq), seqs(ck), seqs(cv), _na_bias(na_rpb[l] * LOG2E))
        xt = _mixout(xt, row(norm_mix[l]), za.reshape(t, -1), ob.reshape(t, -1), oc.reshape(t, -1), w_in_b,
                     *mix_w, l)
        mk, mv = _memkv(mem, row(norm_mem[l]), cross_w_kv_b, l)
        xt = _cross(xt.reshape(bsz, seq, d), row(norm_cross[l]), mk, mv, *cross_w, l).reshape(t, d)
        xt = _ffn(xt, row(norm_ffn2[l]), *ffn2, gf, l == depth - 1, l)
    return xt.reshape(bsz, seq, d)
```

```python
import functools

import numpy as np
import jax
import jax.numpy as jnp
from jax import lax
from jax.experimental import pallas as pl
from jax.experimental.pallas import tpu as pltpu

F32 = jnp.float32
BF16 = jnp.bfloat16

D_MODEL = 1024
EPS = 1e-6
LOG2E = 1.4426950408889634
NEG_INF = -1e30
HEAD_DIM = 64
D_FF = 11 * D_MODEL // 4
CONV_CH = D_MODEL // 4
CONV_WIDTH = 31
WIN_HEADS = (D_MODEL // 2) // HEAD_DIM
WIN_KV_HEADS = 2
WIN_GROUP = WIN_HEADS // WIN_KV_HEADS
WINDOW = 128
BLOCK = 128
T5_BUCKETS = 32
T5_MAX_DIST = 128
NA_HEADS = (D_MODEL // 4) // HEAD_DIM
NA_ROWS = 8
NA_COLS = 16
GRID_W = 64
X_HEADS = 4
X_HEAD_DIM = D_MODEL // X_HEADS
N_BRANCH = 3

WIN_Q = WIN_HEADS * HEAD_DIM
WIN_KV = WIN_KV_HEADS * HEAD_DIM
NA_W = NA_HEADS * HEAD_DIM
OFF_CONV = 0
OFF_BQ = OFF_CONV + 2 * CONV_CH
OFF_BK = OFF_BQ + WIN_Q
OFF_BV = OFF_BK + WIN_KV
OFF_CQ = OFF_BV + WIN_KV
OFF_CK = OFF_CQ + NA_W
OFF_CV = OFF_CK + NA_W
OFF_GATE = OFF_CV + NA_W
IN_WIDTH = OFF_GATE + N_BRANCH * D_MODEL

VMEM_LIMIT_BYTES = 56 * 1024 * 1024
TOKEN_TILE = 1024
FF_CHUNK = 256
MIX_CHUNK = 256
CONV_ROWS = 128
CONV_PAD = 16
NA_Q_ROWS = 16
NA_UNROLL = 8


def _params(*sem, flags=None):
    return pltpu.CompilerParams(dimension_semantics=sem, vmem_limit_bytes=VMEM_LIMIT_BYTES, flags=flags)


def _resident(shape):
    zeros = (0,) * len(shape)
    return pl.BlockSpec(shape, lambda *_: zeros, pipeline_mode=pl.Buffered(1))


def _layer(w, l, cols=None):
    shape = w.shape[1:] if cols is None else w.shape[1:-1] + (cols,)
    zeros = (0,) * len(shape)
    return pl.BlockSpec((None,) + shape, lambda *_: (l,) + zeros, pipeline_mode=pl.Buffered(1))


def _rms(x, g):
    ms = jnp.mean(x * x, axis=-1, keepdims=True)
    return x * lax.rsqrt(ms + EPS) * g


def _dot(a, b):
    return jnp.dot(a, b, preferred_element_type=F32)


def _dot_nt(a, b):
    return lax.dot_general(a, b, (((1,), (1,)), ((), ())), preferred_element_type=F32)


def _ffn_kernel(x_ref, g_ref, wg_ref, wu_ref, wd_ref, gf_ref, o_ref, act_ref, *, final_norm):
    x = x_ref[...]
    h = _rms(x, g_ref[...]).astype(BF16)
    for c in range(D_FF // FF_CHUNK):
        sl = slice(c * FF_CHUNK, (c + 1) * FF_CHUNK)
        a = _dot(h, wg_ref[:, sl])
        b = _dot(h, wu_ref[:, sl])
        act_ref[:, sl] = (a * jax.nn.sigmoid(a) * b).astype(BF16)
    y = x + 0.5 * _dot(act_ref[...], wd_ref[...])
    if final_norm:
        y = _rms(y, gf_ref[...])
    o_ref[...] = y


def _ffn(x, g, wg, wu, wd, gf, final_norm, l):
    t, d = x.shape
    tile = pl.BlockSpec((TOKEN_TILE, d), lambda i: (i, 0))
    return pl.pallas_call(
        functools.partial(_ffn_kernel, final_norm=final_norm),
        grid=(t // TOKEN_TILE,),
        in_specs=[tile, _resident((1, d)), _layer(wg, l), _layer(wu, l), _layer(wd, l), _resident((1, d))],
        out_specs=tile,
        out_shape=jax.ShapeDtypeStruct((t, d), F32),
        scratch_shapes=[pltpu.VMEM((TOKEN_TILE, D_FF), BF16)],
        compiler_params=_params("parallel"),
        name="ffn",
    )(x, g, wg, wu, wd, gf)


def _inproj_kernel(x_ref, g_ref, w_ref, wq_ref, z_ref, bq_ref, bk_ref, bv_ref, cq_ref, ck_ref, cv_ref):
    h = _rms(x_ref[...], g_ref[...]).astype(BF16)
    scale = HEAD_DIM ** -0.5 * LOG2E
    ua = _dot(h, w_ref[:, OFF_CONV:OFF_BQ])
    z_ref[...] = ua[:, :CONV_CH] * jax.nn.sigmoid(ua[:, CONV_CH:])
    bq_ref[...] = (_dot(h, wq_ref[...]) * scale).astype(BF16)
    bkv = _dot(h, w_ref[:, OFF_BK:OFF_CQ])
    bk_ref[...] = bkv[:, :WIN_KV].astype(BF16)
    bv_ref[...] = bkv[:, WIN_KV:].astype(BF16)
    cq_ref[...] = (_dot(h, w_ref[:, OFF_CQ:OFF_CK]) * scale).astype(BF16)
    ck_ref[...] = _dot(h, w_ref[:, OFF_CK:OFF_CV]).astype(BF16)
    cv_ref[...] = _dot(h, w_ref[:, OFF_CV:OFF_GATE]).astype(BF16)


def _inproj(x, g, w, wq, l):
    t, d = x.shape
    widths = (CONV_CH, WIN_Q, WIN_KV, WIN_KV, NA_W, NA_W, NA_W)
    dtypes = (F32,) + (BF16,) * 6

    def tile(n):
        return pl.BlockSpec((TOKEN_TILE, n), lambda i: (i, 0))

    return pl.pallas_call(
        _inproj_kernel,
        grid=(t // TOKEN_TILE,),
        in_specs=[tile(d), _resident((1, d)), _layer(w, l, cols=OFF_GATE), _layer(wq, l)],
        out_specs=[tile(n) for n in widths],
        out_shape=[jax.ShapeDtypeStruct((t, n), dt) for n, dt in zip(widths, dtypes)],
        compiler_params=_params("parallel"),
        name="inproj",
    )(x, g, w, wq)


CONV_SLAB = CONV_ROWS + 2 * CONV_PAD


def _conv_chunk(slab, w, b, lg, lb):
    half = CONV_WIDTH // 2
    lead = CONV_PAD - half
    n_shift = 8
    span = CONV_SLAB - n_shift
    acc = jnp.zeros((CONV_ROWS, CONV_CH), F32) + b
    for s in range(n_shift):
        shifted = slab if s == 0 else pltpu.roll(slab, CONV_SLAB - s, axis=0)
        for a in range(0, span - CONV_ROWS + 1, n_shift):
            k = a + s - lead
            if 0 <= k < CONV_WIDTH:
                acc = acc + shifted[a:a + CONV_ROWS, :] * w[k:k + 1, :]
    mu = jnp.mean(acc, axis=-1, keepdims=True)
    cen = acc - mu
    var = jnp.mean(cen * cen, axis=-1, keepdims=True)
    y = cen * lax.rsqrt(var + EPS) * lg + lb
    return (y * jax.nn.sigmoid(y)).astype(BF16)


def _t5_buckets(rel):
    half = T5_BUCKETS // 2
    max_exact = half // 2
    ret = (rel > 0).astype(np.int32) * half
    n = np.abs(rel)
    large = max_exact + (np.log(np.maximum(n, 1) / max_exact)
                         / np.log(T5_MAX_DIST / max_exact) * (half - max_exact)).astype(np.int32)
    large = np.minimum(large, half - 1)
    return ret + np.where(n < max_exact, n, large)


WIN_PAIRS = WIN_HEADS // WIN_KV_HEADS
WIN_SPAN = BLOCK + 2 * WINDOW
WIN_QB = 8
WIN_UNROLL = 4
assert WIN_KV_HEADS == 2 and 2 * HEAD_DIM == 128


def _win_head_perm():
    heads = [h for t in range(WIN_PAIRS) for h in (t, t + WIN_GROUP)]
    return np.concatenate([np.arange(h * HEAD_DIM, (h + 1) * HEAD_DIM) for h in heads])


def _win_bias(t5_table):
    n = WIN_SPAN + BLOCK
    rel = np.arange(n) - (n // 2 - 1)
    line = jnp.where((np.abs(rel) <= WINDOW)[:, None], t5_table[_t5_buckets(rel)].astype(F32), NEG_INF)
    line = jnp.transpose(line)
    skew = jnp.tile(line, (1, BLOCK))[:, :BLOCK * (n - 1)].reshape(WIN_HEADS, BLOCK, n - 1)
    bias = skew[:, :, BLOCK - 1:BLOCK - 1 + WIN_SPAN]
    bias_t = jnp.transpose(bias, (0, 2, 1))
    key = np.arange(WIN_SPAN)[None, :, None]
    first = jnp.where(key < WINDOW, NEG_INF, bias_t)
    last = jnp.where(key >= WINDOW + BLOCK, NEG_INF, bias_t)
    return jnp.stack([first, bias_t, last])


def _win_kernel(sink_ref, q_ref, kp_ref, kc_ref, kn_ref, vp_ref, vc_ref, vn_ref, bias_ref, o_ref,
                k_scr, vt_scr, *, nblk):
    i = pl.program_id(1)
    k_scr[0] = kp_ref[0]
    vt_scr[0] = vp_ref[0]
    for b in range(WIN_QB):
        k_scr[1 + b] = kc_ref[0, b * BLOCK:(b + 1) * BLOCK, :]
        vt_scr[1 + b] = vc_ref[0, :, b * BLOCK:(b + 1) * BLOCK]
    k_scr[WIN_QB + 1] = kn_ref[0]
    vt_scr[WIN_QB + 1] = vn_ref[0]

    def block(jb, carry):
        g = i * WIN_QB + jb
        variant = jnp.where(g == 0, 0, jnp.where(g == nblk - 1, 2, 1))
        rows = pl.ds(pl.multiple_of(jb * BLOCK, BLOCK), BLOCK)
        k = jnp.concatenate([k_scr[jb], k_scr[jb + 1], k_scr[jb + 2]], axis=0)
        vt = jnp.concatenate([vt_scr[jb], vt_scr[jb + 1], vt_scr[jb + 2]], axis=1)
        low = lax.broadcasted_iota(jnp.int32, k.shape, 1) < HEAD_DIM
        k_half = (jnp.where(low, k, 0), jnp.where(low, 0, k))
        ch = lax.broadcasted_iota(jnp.int32, vt.shape, 0)
        den_row = (HEAD_DIM, 0)
        vt_half = (jnp.where(ch < HEAD_DIM, vt, jnp.where(ch == den_row[0], 1, 0).astype(BF16)),
                   jnp.where(ch >= HEAD_DIM, vt, jnp.where(ch == den_row[1], 1, 0).astype(BF16)))
        for t in range(WIN_PAIRS):
            qp = q_ref[0, rows, t * 128:(t + 1) * 128]
            halves = []
            for j in range(WIN_KV_HEADS):
                h = t + j * WIN_GROUP
                sink = sink_ref[h]
                s = _dot_nt(k_half[j], qp) + bias_ref[variant, h]
                m = jnp.maximum(jnp.max(s, axis=0, keepdims=True), sink)
                p = jnp.exp2(s - m).astype(BF16)
                o = _dot(vt_half[j], p)
                den = o[den_row[j]:den_row[j] + 1] + jnp.exp2(sink - m)
                halves.append(o[j * HEAD_DIM:(j + 1) * HEAD_DIM] * (1.0 / den))
            o_ref[0, rows, t * 128:(t + 1) * 128] = jnp.concatenate(halves, axis=0).T.astype(BF16)
        return carry

    lax.fori_loop(0, WIN_QB, block, 0, unroll=WIN_UNROLL)


def _win_attn(q, k, vt, sink, bias):
    bsz, seq, _ = q.shape
    nblk = seq // BLOCK
    assert nblk >= 2 and nblk % WIN_QB == 0

    def edge(shift):
        return lambda b, i: jnp.clip(i * WIN_QB + shift, 0, nblk - 1)

    def k_edge(shift):
        return pl.BlockSpec((1, BLOCK, WIN_KV), lambda b, i: (b, edge(shift)(b, i), 0))

    def v_edge(shift):
        return pl.BlockSpec((1, WIN_KV, BLOCK), lambda b, i: (b, 0, edge(shift)(b, i)))

    qspec = pl.BlockSpec((1, WIN_QB * BLOCK, WIN_Q), lambda b, i: (b, i, 0))
    return pl.pallas_call(
        functools.partial(_win_kernel, nblk=nblk),
        grid=(bsz, nblk // WIN_QB),
        in_specs=[pl.BlockSpec(memory_space=pltpu.SMEM), qspec,
                  k_edge(-1), pl.BlockSpec((1, WIN_QB * BLOCK, WIN_KV), lambda b, i: (b, i, 0)), k_edge(WIN_QB),
                  v_edge(-1), pl.BlockSpec((1, WIN_KV, WIN_QB * BLOCK), lambda b, i: (b, 0, i)), v_edge(WIN_QB),
                  _resident(bias.shape)],
        out_specs=qspec,
        out_shape=jax.ShapeDtypeStruct((bsz, seq, WIN_Q), BF16),
        scratch_shapes=[pltpu.VMEM((WIN_QB + 2, BLOCK, WIN_KV), BF16),
                        pltpu.VMEM((WIN_QB + 2, WIN_KV, BLOCK), BF16)],
        compiler_params=_params("parallel", "parallel"),
        name="win_attn",
    )(sink, q, k, k, k, vt, vt, vt, bias)


def _na_bias(rpb):
    c = np.arange(GRID_W)
    col_start = np.clip(c - NA_COLS // 2, 0, GRID_W - NA_COLS)
    col = np.arange(GRID_W)
    inside = (col[None, :] >= col_start[:, None]) & (col[None, :] < col_start[:, None] + NA_COLS)
    b = rpb.astype(F32)
    n = 2 * GRID_W
    lead = GRID_W - NA_COLS
    line = jnp.pad(b, ((0, 0),) * 2 + ((lead, n - lead - (2 * NA_COLS - 1)),))
    skew = jnp.tile(line, (1, 1, GRID_W))[..., :GRID_W * (n - 1)]
    skew = skew.reshape(b.shape[:2] + (GRID_W, n - 1))[..., GRID_W - 1:2 * GRID_W - 1]
    b = jnp.where(inside[None, None], skew, NEG_INF)
    b = b.reshape((NA_HEADS // 2, 2) + b.shape[1:])
    b = jnp.transpose(b, (0, 2, 4, 1, 3))
    return b.reshape(NA_HEADS // 2, (2 * NA_ROWS - 1) * GRID_W, 2 * GRID_W)


def _na_kernel(q_ref, k_ref, v_ref, bias_ref, o_ref, *, rows):
    j = pl.program_id(1)
    win = NA_ROWS * GRID_W
    low = lax.broadcasted_iota(jnp.int32, (GRID_W, 2 * HEAD_DIM), 1) < HEAD_DIM

    def row(t, carry):
        r = j * NA_Q_ROWS + t
        rs = jnp.clip(r - NA_ROWS // 2, 0, rows - NA_ROWS)
        keys = pl.ds(pl.multiple_of(rs * GRID_W, GRID_W), win)
        qrows = pl.ds(pl.multiple_of(t * GRID_W, GRID_W), GRID_W)
        brows = pl.ds(pl.multiple_of((NA_ROWS - 1 - (r - rs)) * GRID_W, GRID_W), win)
        for tile in range(NA_HEADS // 2):
            lanes = slice(tile * 2 * HEAD_DIM, (tile + 1) * 2 * HEAD_DIM)
            q = q_ref[0, qrows, lanes]
            q_bd = jnp.concatenate([jnp.where(low, q, 0), jnp.where(low, 0, q)], axis=0)
            s = _dot_nt(k_ref[0, keys, lanes], q_bd) + bias_ref[tile, brows, :]
            m = jnp.max(s, axis=0, keepdims=True)
            p = jnp.exp2(s - m)
            den = jnp.sum(p, axis=0, keepdims=True)
            o = lax.dot_general(v_ref[0, keys, lanes], p.astype(BF16), (((0,), (0,)), ((), ())),
                                preferred_element_type=F32)
            ot = (o * (1.0 / den)).T
            o_ref[0, qrows, lanes] = jnp.where(low, ot[:GRID_W], ot[GRID_W:]).astype(BF16)
        return carry

    lax.fori_loop(0, NA_Q_ROWS, row, 0, unroll=NA_UNROLL)


def _na_attn(q, k, v, bias):
    bsz, seq, _ = q.shape
    rows = seq // GRID_W
    qspec = pl.BlockSpec((1, NA_Q_ROWS * GRID_W, NA_W), lambda b, j: (b, j, 0))
    kvspec = pl.BlockSpec((1, seq, NA_W), lambda b, j: (b, 0, 0))
    return pl.pallas_call(
        functools.partial(_na_kernel, rows=rows),
        grid=(bsz, rows // NA_Q_ROWS),
        in_specs=[qspec, kvspec, kvspec, _resident(bias.shape)],
        out_specs=qspec,
        out_shape=jax.ShapeDtypeStruct((bsz, seq, NA_W), BF16),
        compiler_params=_params("parallel", "arbitrary"),
        name="na_attn",
    )(q, k, v, bias)


def _mixout_kernel(x_ref, g_ref, zprev_ref, z_ref, znext_ref, cw_ref, cb_ref, clg_ref, clb_ref,
                   ob_ref, oc_ref, wga_ref, wgb_ref, wgc_ref, wa_ref, wb_ref, wc_ref, wo_ref,
                   o_ref, y_ref, *, tiles_per_seq):
    x = x_ref[...]
    h = _rms(x, g_ref[...]).astype(BF16)
    chunks = [slice(c * MIX_CHUNK, (c + 1) * MIX_CHUNK) for c in range(D_MODEL // MIX_CHUNK)]

    def gated(wg_ref, b_ref, w_ref, sl):
        return jax.nn.sigmoid(_dot(h, wg_ref[:, sl])) * _dot(b_ref[...], w_ref[:, sl])

    pos = pl.program_id(0) % tiles_per_seq
    before = jnp.where(pos != 0, zprev_ref[...], 0.0)
    after = jnp.where(pos != tiles_per_seq - 1, znext_ref[...], 0.0)
    n_chunks = TOKEN_TILE // CONV_ROWS
    conv_args = (cw_ref[...], cb_ref[...], clg_ref[...], clb_ref[...])

    def conv(c):
        lo, hi = c * CONV_ROWS - CONV_PAD, (c + 1) * CONV_ROWS + CONV_PAD
        parts = ([before] if c == 0 else []) + [z_ref[max(lo, 0):min(hi, TOKEN_TILE), :]] \
            + ([after] if c == n_chunks - 1 else [])
        slab = parts[0] if len(parts) == 1 else jnp.concatenate(parts, axis=0)
        return _conv_chunk(slab, *conv_args)

    za = jnp.concatenate([conv(c) for c in range(n_chunks)], axis=0)
    for sl in chunks:
        y = gated(wgb_ref, ob_ref, wb_ref, sl) + gated(wgc_ref, oc_ref, wc_ref, sl)
        y = y + jax.nn.sigmoid(_dot(h, wga_ref[:, sl])) * _dot(za, wa_ref[:, sl])
        y_ref[:, sl] = y.astype(BF16)
    o_ref[...] = x + _dot(y_ref[...], wo_ref[...])


def _mixout(x, g, z, conv_p, ob, oc, w_in, wa, wb, wc, wo, l, seq):
    t, d = x.shape
    assert OFF_GATE % d == 0 and seq % TOKEN_TILE == 0 and TOKEN_TILE % CONV_ROWS == 0
    halo_per_tile = TOKEN_TILE // CONV_PAD
    n_halo = t // CONV_PAD

    def tile(n):
        return pl.BlockSpec((TOKEN_TILE, n), lambda i: (i, 0))

    def gate_cols(c):
        blk = OFF_GATE // d + c
        return pl.BlockSpec((None, d, d), lambda i: (l, 0, blk), pipeline_mode=pl.Buffered(1))

    z_before = pl.BlockSpec((CONV_PAD, CONV_CH), lambda i: (jnp.maximum(i * halo_per_tile - 1, 0), 0))
    z_after = pl.BlockSpec((CONV_PAD, CONV_CH), lambda i: (jnp.minimum((i + 1) * halo_per_tile, n_halo - 1), 0))
    return pl.pallas_call(
        functools.partial(_mixout_kernel, tiles_per_seq=seq // TOKEN_TILE),
        grid=(t // TOKEN_TILE,),
        in_specs=[tile(d), _resident((1, d)), z_before, tile(CONV_CH), z_after]
                 + [_resident(p.shape) for p in conv_p]
                 + [tile(WIN_Q), tile(NA_W), gate_cols(0), gate_cols(1), gate_cols(2),
                    _layer(wa, l), _layer(wb, l), _layer(wc, l), _layer(wo, l)],
        out_specs=tile(d),
        out_shape=jax.ShapeDtypeStruct((t, d), F32),
        scratch_shapes=[pltpu.VMEM((TOKEN_TILE, d), BF16)],
        compiler_params=_params("parallel"),
        name="mixout",
    )(x, g, z, z, z, *conv_p, ob, oc, w_in, w_in, w_in, wa, wb, wc, wo)


def _memkv_kernel(m_ref, g_ref, w_ref, k_ref, v_ref):
    h = _rms(m_ref[0], g_ref[...]).astype(BF16)
    k_ref[0] = _dot(h, w_ref[:, :D_MODEL]).astype(BF16)
    v_ref[0] = _dot(h, w_ref[:, D_MODEL:]).astype(BF16)


def _memkv(mem, g, w, l):
    bsz, mlen, d = mem.shape
    blk = pl.BlockSpec((1, mlen, d), lambda b: (b, 0, 0))
    return pl.pallas_call(
        _memkv_kernel,
        grid=(bsz,),
        in_specs=[blk, _resident((1, d)), _layer(w, l)],
        out_specs=[blk, blk],
        out_shape=[jax.ShapeDtypeStruct((bsz, mlen, d), BF16)] * 2,
        compiler_params=_params("parallel"),
        name="memkv",
    )(mem, g, w)


def _cross_kernel(x_ref, g_ref, k_ref, v_ref, wq_ref, wo_ref, o_ref):
    x = x_ref[0]
    h = _rms(x, g_ref[...]).astype(BF16)
    scale = X_HEAD_DIM ** -0.5
    q = (_dot(h, wq_ref[...]) * scale).astype(BF16)
    k = k_ref[0]
    v = v_ref[0]
    outs = []
    for hd in range(X_HEADS):
        hs = slice(hd * X_HEAD_DIM, (hd + 1) * X_HEAD_DIM)
        s = _dot_nt(q[:, hs], k[:, hs])
        m = jnp.max(s, axis=-1, keepdims=True)
        p = jnp.exp(s - m)
        den = jnp.sum(p, axis=-1, keepdims=True)
        outs.append((_dot(p.astype(BF16), v[:, hs]) / den).astype(BF16))
    o = jnp.concatenate(outs, axis=-1)
    o_ref[0] = x + _dot(o, wo_ref[...])


def _cross(x, g, k, v, wq, wo, l):
    bsz, seq, d = x.shape
    mlen = k.shape[1]
    tile = pl.BlockSpec((1, TOKEN_TILE, d), lambda b, i: (b, i, 0))
    kv = pl.BlockSpec((1, mlen, d), lambda b, i: (b, 0, 0))
    return pl.pallas_call(
        _cross_kernel,
        grid=(bsz, seq // TOKEN_TILE),
        in_specs=[tile, _resident((1, d)), kv, kv, _layer(wq, l), _layer(wo, l)],
        out_specs=tile,
        out_shape=jax.ShapeDtypeStruct((bsz, seq, d), F32),
        compiler_params=_params("parallel", "arbitrary"),
        name="cross",
    )(x, g, k, v, wq, wo)


def kernel(x, mem, norm_ffn1, ffn1_w_gate, ffn1_w_up, ffn1_w_down, norm_mix, w_in, conv_dw_w, conv_dw_b, conv_ln_g, conv_ln_b, conv_w_out, win_sink, t5_bias, win_w_out, na_rpb, na_w_out, w_out, norm_cross, norm_mem, cross_w_q, cross_w_kv, cross_w_o, norm_ffn2, ffn2_w_gate, ffn2_w_up, ffn2_w_down, norm_final):
    bsz, seq, d = x.shape
    depth = w_in.shape[0]
    t = bsz * seq
    assert d == D_MODEL and t % TOKEN_TILE == 0 and seq % TOKEN_TILE == 0
    assert seq % (NA_Q_ROWS * GRID_W) == 0 and seq // GRID_W >= NA_ROWS and seq % CONV_ROWS == 0

    def row(v):
        return v.reshape(1, -1).astype(F32)

    def bf(w):
        return w.astype(BF16)

    win_bias = _win_bias(t5_bias * LOG2E)
    win_heads = [h for tl in range(WIN_PAIRS) for h in (tl, tl + WIN_GROUP)]
    gf = row(norm_final)
    ffn1 = (bf(ffn1_w_gate), bf(ffn1_w_up), bf(ffn1_w_down))
    ffn2 = (bf(ffn2_w_gate), bf(ffn2_w_up), bf(ffn2_w_down))
    w_in_b = bf(w_in)
    w_bq = jnp.concatenate(
        [w_in_b[:, :, OFF_BQ + h * HEAD_DIM:OFF_BQ + (h + 1) * HEAD_DIM] for h in win_heads], axis=2)
    win_w_out_b = jnp.concatenate(
        [bf(win_w_out[:, h * HEAD_DIM:(h + 1) * HEAD_DIM]) for h in win_heads], axis=1)
    mix_w = (bf(conv_w_out), win_w_out_b, bf(na_w_out), bf(w_out))
    cross_w = (bf(cross_w_q), bf(cross_w_o))
    cross_w_kv_b = bf(cross_w_kv)
    xt = x.reshape(t, d)
    for l in range(depth):
        xt = _ffn(xt, row(norm_ffn1[l]), *ffn1, gf, False, l)
        z, bq, bk, bv, cq, ck, cv = _inproj(xt, row(norm_mix[l]), w_in_b, w_bq, l)

        def seqs(a):
            return a.reshape(bsz, seq, a.shape[-1])

        conv_p = (conv_dw_w[l].astype(F32), row(conv_dw_b[l]), row(conv_ln_g[l]), row(conv_ln_b[l]))
        ob =_win_attn(seqs(bq), seqs(bk), jnp.swapaxes(seqs(bv), 1, 2), win_sink[l].astype(F32) * LOG2E, win_bias)
        oc = _na_attn(seqs(cq), seqs(ck), seqs(cv), _na_bias(na_rpb[l] * LOG2E))
        xt = _mixout(xt, row(norm_mix[l]), z, conv_p, ob.reshape(t, -1), oc.reshape(t, -1), w_in_b,
                     *mix_w, l, seq)
        mk, mv = _memkv(mem, row(norm_mem[l]), cross_w_kv_b, l)
        xt = _cross(xt.reshape(bsz, seq, d), row(norm_cross[l]), mk, mv, *cross_w, l).reshape(t, d)
        xt = _ffn(xt, row(norm_ffn2[l]), *ffn2, gf, l == depth - 1, l)
    return xt.reshape(bsz, seq, d)
```

```python
import functools

import numpy as np
import jax
import jax.numpy as jnp
from jax import lax
from jax.experimental import pallas as pl
from jax.experimental.pallas import tpu as pltpu

F32 = jnp.float32
BF16 = jnp.bfloat16

D_MODEL = 1024
EPS = 1e-6
LOG2E = 1.4426950408889634
NEG_INF = -1e30
HEAD_DIM = 64
D_FF = 11 * D_MODEL // 4
CONV_CH = D_MODEL // 4
CONV_WIDTH = 31
WIN_HEADS = (D_MODEL // 2) // HEAD_DIM
WIN_KV_HEADS = 2
WIN_GROUP = WIN_HEADS // WIN_KV_HEADS
WINDOW = 128
BLOCK = 128
T5_BUCKETS = 32
T5_MAX_DIST = 128
NA_HEADS = (D_MODEL // 4) // HEAD_DIM
NA_ROWS = 8
NA_COLS = 16
GRID_W = 64
X_HEADS = 4
X_HEAD_DIM = D_MODEL // X_HEADS
N_BRANCH = 3

WIN_Q = WIN_HEADS * HEAD_DIM
WIN_KV = WIN_KV_HEADS * HEAD_DIM
NA_W = NA_HEADS * HEAD_DIM
OFF_CONV = 0
OFF_BQ = OFF_CONV + 2 * CONV_CH
OFF_BK = OFF_BQ + WIN_Q
OFF_BV = OFF_BK + WIN_KV
OFF_CQ = OFF_BV + WIN_KV
OFF_CK = OFF_CQ + NA_W
OFF_CV = OFF_CK + NA_W
OFF_GATE = OFF_CV + NA_W
IN_WIDTH = OFF_GATE + N_BRANCH * D_MODEL

VMEM_LIMIT_BYTES = 56 * 1024 * 1024
TOKEN_TILE = 1024
FF_CHUNK = 256
MIX_CHUNK = 256
CONV_ROWS = 128
CONV_PAD = 16
NA_Q_ROWS = 16
NA_UNROLL = 8


def _params(*sem, flags=None):
    return pltpu.CompilerParams(dimension_semantics=sem, vmem_limit_bytes=VMEM_LIMIT_BYTES, flags=flags)


def _resident(shape):
    zeros = (0,) * len(shape)
    return pl.BlockSpec(shape, lambda *_: zeros, pipeline_mode=pl.Buffered(1))


def _layer(w, l, cols=None):
    shape = w.shape[1:] if cols is None else w.shape[1:-1] + (cols,)
    zeros = (0,) * len(shape)
    return pl.BlockSpec((None,) + shape, lambda *_: (l,) + zeros, pipeline_mode=pl.Buffered(1))


def _side_specs(side, steps):
    packed_rows = 16
    for a in side:
        assert a.ndim == 2 and a.shape[0] % (steps * packed_rows) == 0
    return [pl.BlockSpec((a.shape[0] // steps, a.shape[1]), lambda i: (i, 0)) for a in side]


def _side_shapes(side):
    return [jax.ShapeDtypeStruct(a.shape, BF16) for a in side]


def _cast_side(src_refs, dst_refs):
    for src, dst in zip(src_refs, dst_refs):
        dst[...] = src[...].astype(BF16)


def _rms(x, g):
    ms = jnp.mean(x * x, axis=-1, keepdims=True)
    return x * lax.rsqrt(ms + EPS) * g


def _dot(a, b):
    return jnp.dot(a, b, preferred_element_type=F32)


def _dot_nt(a, b):
    return lax.dot_general(a, b, (((1,), (1,)), ((), ())), preferred_element_type=F32)


def _ffn_kernel(*refs, final_norm, n_side):
    x_ref, g_ref, wg_ref, wu_ref, wd_ref, gf_ref = refs[:6]
    o_ref, act_ref = refs[6 + n_side], refs[-1]
    _cast_side(refs[6:6 + n_side], refs[7 + n_side:7 + 2 * n_side])
    x = x_ref[...]
    h = _rms(x, g_ref[...]).astype(BF16)
    for c in range(D_FF // FF_CHUNK):
        sl = slice(c * FF_CHUNK, (c + 1) * FF_CHUNK)
        a = _dot(h, wg_ref[:, sl])
        b = _dot(h, wu_ref[:, sl])
        act_ref[:, sl] = (a * jax.nn.sigmoid(a) * b).astype(BF16)
    y = x + 0.5 * _dot(act_ref[...], wd_ref[...])
    if final_norm:
        y = _rms(y, gf_ref[...])
    o_ref[...] = y


def _ffn(x, g, wg, wu, wd, gf, final_norm, l, side=()):
    t, d = x.shape
    steps = t // TOKEN_TILE
    tile = pl.BlockSpec((TOKEN_TILE, d), lambda i: (i, 0))
    out = pl.pallas_call(
        functools.partial(_ffn_kernel, final_norm=final_norm, n_side=len(side)),
        grid=(steps,),
        in_specs=[tile, _resident((1, d)), _layer(wg, l), _layer(wu, l), _layer(wd, l), _resident((1, d))]
                 + _side_specs(side, steps),
        out_specs=[tile] + _side_specs(side, steps),
        out_shape=[jax.ShapeDtypeStruct((t, d), F32)] + _side_shapes(side),
        scratch_shapes=[pltpu.VMEM((TOKEN_TILE, D_FF), BF16)],
        compiler_params=_params("parallel"),
        name="ffn",
    )(x, g, wg, wu, wd, gf, *side)
    return out[0], out[1:]


def _inproj_kernel(*refs, n_side):
    x_ref, g_ref, w_ref, wq_ref = refs[:4]
    z_ref, bq_ref, bk_ref, bv_ref, cq_ref, ck_ref, cv_ref = refs[4 + n_side:11 + n_side]
    _cast_side(refs[4:4 + n_side], refs[11 + n_side:])
    h = _rms(x_ref[...], g_ref[...]).astype(BF16)
    scale = HEAD_DIM ** -0.5 * LOG2E
    ua = _dot(h, w_ref[:, OFF_CONV:OFF_BQ])
    z_ref[...] = ua[:, :CONV_CH] * jax.nn.sigmoid(ua[:, CONV_CH:])
    bq_ref[...] = (_dot(h, wq_ref[...]) * scale).astype(BF16)
    bkv = _dot(h, w_ref[:, OFF_BK:OFF_CQ])
    bk_ref[...] = bkv[:, :WIN_KV].astype(BF16)
    bv_ref[...] = bkv[:, WIN_KV:].astype(BF16)
    cq_ref[...] = (_dot(h, w_ref[:, OFF_CQ:OFF_CK]) * scale).astype(BF16)
    ck_ref[...] = _dot(h, w_ref[:, OFF_CK:OFF_CV]).astype(BF16)
    cv_ref[...] = _dot(h, w_ref[:, OFF_CV:OFF_GATE]).astype(BF16)


def _inproj(x, g, w, wq, l, side=()):
    t, d = x.shape
    steps = t // TOKEN_TILE
    widths = (CONV_CH, WIN_Q, WIN_KV, WIN_KV, NA_W, NA_W, NA_W)
    dtypes = (F32,) + (BF16,) * 6

    def tile(n):
        return pl.BlockSpec((TOKEN_TILE, n), lambda i: (i, 0))

    out = pl.pallas_call(
        functools.partial(_inproj_kernel, n_side=len(side)),
        grid=(steps,),
        in_specs=[tile(d), _resident((1, d)), _layer(w, l, cols=OFF_GATE), _layer(wq, l)]
                 + _side_specs(side, steps),
        out_specs=[tile(n) for n in widths] + _side_specs(side, steps),
        out_shape=[jax.ShapeDtypeStruct((t, n), dt) for n, dt in zip(widths, dtypes)] + _side_shapes(side),
        compiler_params=_params("parallel"),
        name="inproj",
    )(x, g, w, wq, *side)
    return out[:len(widths)], out[len(widths):]


CONV_SLAB = CONV_ROWS + 2 * CONV_PAD


def _conv_chunk(slab, w, b, lg, lb):
    half = CONV_WIDTH // 2
    lead = CONV_PAD - half
    n_shift = 8
    span = CONV_SLAB - n_shift
    acc = jnp.zeros((CONV_ROWS, CONV_CH), F32) + b
    for s in range(n_shift):
        shifted = slab if s == 0 else pltpu.roll(slab, CONV_SLAB - s, axis=0)
        for a in range(0, span - CONV_ROWS + 1, n_shift):
            k = a + s - lead
            if 0 <= k < CONV_WIDTH:
                acc = acc + shifted[a:a + CONV_ROWS, :] * w[k:k + 1, :]
    mu = jnp.mean(acc, axis=-1, keepdims=True)
    cen = acc - mu
    var = jnp.mean(cen * cen, axis=-1, keepdims=True)
    y = cen * lax.rsqrt(var + EPS) * lg + lb
    return (y * jax.nn.sigmoid(y)).astype(BF16)


def _t5_buckets(rel):
    half = T5_BUCKETS // 2
    max_exact = half // 2
    ret = (rel > 0).astype(np.int32) * half
    n = np.abs(rel)
    large = max_exact + (np.log(np.maximum(n, 1) / max_exact)
                         / np.log(T5_MAX_DIST / max_exact) * (half - max_exact)).astype(np.int32)
    large = np.minimum(large, half - 1)
    return ret + np.where(n < max_exact, n, large)


WIN_PAIRS = WIN_HEADS // WIN_KV_HEADS
WIN_SPAN = BLOCK + 2 * WINDOW
WIN_QB = 8
WIN_UNROLL = 4
assert WIN_KV_HEADS == 2 and 2 * HEAD_DIM == 128


def _win_head_perm():
    heads = [h for t in range(WIN_PAIRS) for h in (t, t + WIN_GROUP)]
    return np.concatenate([np.arange(h * HEAD_DIM, (h + 1) * HEAD_DIM) for h in heads])


def _win_bias(t5_table):
    n = WIN_SPAN + BLOCK
    rel = np.arange(n) - (n // 2 - 1)
    line = jnp.where((np.abs(rel) <= WINDOW)[:, None], t5_table[_t5_buckets(rel)].astype(F32), NEG_INF)
    line = jnp.transpose(line)
    skew = jnp.tile(line, (1, BLOCK))[:, :BLOCK * (n - 1)].reshape(WIN_HEADS, BLOCK, n - 1)
    bias = skew[:, :, BLOCK - 1:BLOCK - 1 + WIN_SPAN]
    bias_t = jnp.transpose(bias, (0, 2, 1))
    key = np.arange(WIN_SPAN)[None, :, None]
    first = jnp.where(key < WINDOW, NEG_INF, bias_t)
    last = jnp.where(key >= WINDOW + BLOCK, NEG_INF, bias_t)
    return jnp.stack([first, bias_t, last])


def _win_kernel(sink_ref, q_ref, kp_ref, kc_ref, kn_ref, vp_ref, vc_ref, vn_ref, bias_ref, o_ref,
                k_scr, vt_scr, *, nblk):
    i = pl.program_id(1)
    k_scr[0] = kp_ref[0]
    vt_scr[0] = vp_ref[0]
    for b in range(WIN_QB):
        k_scr[1 + b] = kc_ref[0, b * BLOCK:(b + 1) * BLOCK, :]
        vt_scr[1 + b] = vc_ref[0, :, b * BLOCK:(b + 1) * BLOCK]
    k_scr[WIN_QB + 1] = kn_ref[0]
    vt_scr[WIN_QB + 1] = vn_ref[0]

    def block(jb, carry):
        g = i * WIN_QB + jb
        variant = jnp.where(g == 0, 0, jnp.where(g == nblk - 1, 2, 1))
        rows = pl.ds(pl.multiple_of(jb * BLOCK, BLOCK), BLOCK)
        k = jnp.concatenate([k_scr[jb], k_scr[jb + 1], k_scr[jb + 2]], axis=0)
        vt = jnp.concatenate([vt_scr[jb], vt_scr[jb + 1], vt_scr[jb + 2]], axis=1)
        low = lax.broadcasted_iota(jnp.int32, k.shape, 1) < HEAD_DIM
        k_half = (jnp.where(low, k, 0), jnp.where(low, 0, k))
        ch = lax.broadcasted_iota(jnp.int32, vt.shape, 0)
        den_row = (HEAD_DIM, 0)
        vt_half = (jnp.where(ch < HEAD_DIM, vt, jnp.where(ch == den_row[0], 1, 0).astype(BF16)),
                   jnp.where(ch >= HEAD_DIM, vt, jnp.where(ch == den_row[1], 1, 0).astype(BF16)))
        for t in range(WIN_PAIRS):
            qp = q_ref[0, rows, t * 128:(t + 1) * 128]
            halves = []
            for j in range(WIN_KV_HEADS):
                h = t + j * WIN_GROUP
                sink = sink_ref[h]
                s = _dot_nt(k_half[j], qp) + bias_ref[variant, h]
                m = jnp.maximum(jnp.max(s, axis=0, keepdims=True), sink)
                p = jnp.exp2(s - m).astype(BF16)
                o = _dot(vt_half[j], p)
                den = o[den_row[j]:den_row[j] + 1] + jnp.exp2(sink - m)
                halves.append(o[j * HEAD_DIM:(j + 1) * HEAD_DIM] * (1.0 / den))
            o_ref[0, rows, t * 128:(t + 1) * 128] = jnp.concatenate(halves, axis=0).T.astype(BF16)
        return carry

    lax.fori_loop(0, WIN_QB, block, 0, unroll=WIN_UNROLL)


def _win_attn(q, k, vt, sink, bias):
    bsz, seq, _ = q.shape
    nblk = seq // BLOCK
    assert nblk >= 2 and nblk % WIN_QB == 0

    def edge(shift):
        return lambda b, i: jnp.clip(i * WIN_QB + shift, 0, nblk - 1)

    def k_edge(shift):
        return pl.BlockSpec((1, BLOCK, WIN_KV), lambda b, i: (b, edge(shift)(b, i), 0))

    def v_edge(shift):
        return pl.BlockSpec((1, WIN_KV, BLOCK), lambda b, i: (b, 0, edge(shift)(b, i)))

    qspec = pl.BlockSpec((1, WIN_QB * BLOCK, WIN_Q), lambda b, i: (b, i, 0))
    return pl.pallas_call(
        functools.partial(_win_kernel, nblk=nblk),
        grid=(bsz, nblk // WIN_QB),
        in_specs=[pl.BlockSpec(memory_space=pltpu.SMEM), qspec,
                  k_edge(-1), pl.BlockSpec((1, WIN_QB * BLOCK, WIN_KV), lambda b, i: (b, i, 0)), k_edge(WIN_QB),
                  v_edge(-1), pl.BlockSpec((1, WIN_KV, WIN_QB * BLOCK), lambda b, i: (b, 0, i)), v_edge(WIN_QB),
                  _resident(bias.shape)],
        out_specs=qspec,
        out_shape=jax.ShapeDtypeStruct((bsz, seq, WIN_Q), BF16),
        scratch_shapes=[pltpu.VMEM((WIN_QB + 2, BLOCK, WIN_KV), BF16),
                        pltpu.VMEM((WIN_QB + 2, WIN_KV, BLOCK), BF16)],
        compiler_params=_params("parallel", "parallel"),
        name="win_attn",
    )(sink, q, k, k, k, vt, vt, vt, bias)


def _na_bias(rpb):
    c = np.arange(GRID_W)
    col_start = np.clip(c - NA_COLS // 2, 0, GRID_W - NA_COLS)
    col = np.arange(GRID_W)
    inside = (col[None, :] >= col_start[:, None]) & (col[None, :] < col_start[:, None] + NA_COLS)
    b = rpb.astype(F32)
    n = 2 * GRID_W
    lead = GRID_W - NA_COLS
    line = jnp.pad(b, ((0, 0),) * 2 + ((lead, n - lead - (2 * NA_COLS - 1)),))
    skew = jnp.tile(line, (1, 1, GRID_W))[..., :GRID_W * (n - 1)]
    skew = skew.reshape(b.shape[:2] + (GRID_W, n - 1))[..., GRID_W - 1:2 * GRID_W - 1]
    b = jnp.where(inside[None, None], skew, NEG_INF)
    b = b.reshape((NA_HEADS // 2, 2) + b.shape[1:])
    b = jnp.transpose(b, (0, 2, 4, 1, 3))
    return b.reshape(NA_HEADS // 2, (2 * NA_ROWS - 1) * GRID_W, 2 * GRID_W)


def _na_kernel(q_ref, k_ref, v_ref, bias_ref, o_ref, *, rows):
    j = pl.program_id(1)
    win = NA_ROWS * GRID_W
    low = lax.broadcasted_iota(jnp.int32, (GRID_W, 2 * HEAD_DIM), 1) < HEAD_DIM

    def row(t, carry):
        r = j * NA_Q_ROWS + t
        rs = jnp.clip(r - NA_ROWS // 2, 0, rows - NA_ROWS)
        keys = pl.ds(pl.multiple_of(rs * GRID_W, GRID_W), win)
        qrows = pl.ds(pl.multiple_of(t * GRID_W, GRID_W), GRID_W)
        brows = pl.ds(pl.multiple_of((NA_ROWS - 1 - (r - rs)) * GRID_W, GRID_W), win)
        for tile in range(NA_HEADS // 2):
            lanes = slice(tile * 2 * HEAD_DIM, (tile + 1) * 2 * HEAD_DIM)
            q = q_ref[0, qrows, lanes]
            q_bd = jnp.concatenate([jnp.where(low, q, 0), jnp.where(low, 0, q)], axis=0)
            s = _dot_nt(k_ref[0, keys, lanes], q_bd) + bias_ref[tile, brows, :]
            m = jnp.max(s, axis=0, keepdims=True)
            p = jnp.exp2(s - m)
            den = jnp.sum(p, axis=0, keepdims=True)
            o = lax.dot_general(v_ref[0, keys, lanes], p.astype(BF16), (((0,), (0,)), ((), ())),
                                preferred_element_type=F32)
            ot = (o * (1.0 / den)).T
            o_ref[0, qrows, lanes] = jnp.where(low, ot[:GRID_W], ot[GRID_W:]).astype(BF16)
        return carry

    lax.fori_loop(0, NA_Q_ROWS, row, 0, unroll=NA_UNROLL)


def _na_attn(q, k, v, bias):
    bsz, seq, _ = q.shape
    rows = seq // GRID_W
    qspec = pl.BlockSpec((1, NA_Q_ROWS * GRID_W, NA_W), lambda b, j: (b, j, 0))
    kvspec = pl.BlockSpec((1, seq, NA_W), lambda b, j: (b, 0, 0))
    return pl.pallas_call(
        functools.partial(_na_kernel, rows=rows),
        grid=(bsz, rows // NA_Q_ROWS),
        in_specs=[qspec, kvspec, kvspec, _resident(bias.shape)],
        out_specs=qspec,
        out_shape=jax.ShapeDtypeStruct((bsz, seq, NA_W), BF16),
        compiler_params=_params("parallel", "arbitrary"),
        name="na_attn",
    )(q, k, v, bias)


def _mixout_kernel(x_ref, g_ref, zprev_ref, z_ref, znext_ref, cw_ref, cb_ref, clg_ref, clb_ref,
                   ob_ref, oc_ref, wga_ref, wgb_ref, wgc_ref, wa_ref, wb_ref, wc_ref, wo_ref,
                   o_ref, y_ref, *, tiles_per_seq):
    x = x_ref[...]
    h = _rms(x, g_ref[...]).astype(BF16)
    chunks = [slice(c * MIX_CHUNK, (c + 1) * MIX_CHUNK) for c in range(D_MODEL // MIX_CHUNK)]

    def gated(wg_ref, b_ref, w_ref, sl):
        return jax.nn.sigmoid(_dot(h, wg_ref[:, sl])) * _dot(b_ref[...], w_ref[:, sl])

    pos = pl.program_id(0) % tiles_per_seq
    before = jnp.where(pos != 0, zprev_ref[...], 0.0)
    after = jnp.where(pos != tiles_per_seq - 1, znext_ref[...], 0.0)
    n_chunks = TOKEN_TILE // CONV_ROWS
    conv_args = (cw_ref[...], cb_ref[...], clg_ref[...], clb_ref[...])

    def conv(c):
        lo, hi = c * CONV_ROWS - CONV_PAD, (c + 1) * CONV_ROWS + CONV_PAD
        parts = ([before] if c == 0 else []) + [z_ref[max(lo, 0):min(hi, TOKEN_TILE), :]] \
            + ([after] if c == n_chunks - 1 else [])
        slab = parts[0] if len(parts) == 1 else jnp.concatenate(parts, axis=0)
        return _conv_chunk(slab, *conv_args)

    za = jnp.concatenate([conv(c) for c in range(n_chunks)], axis=0)
    for sl in chunks:
        y = gated(wgb_ref, ob_ref, wb_ref, sl) + gated(wgc_ref, oc_ref, wc_ref, sl)
        y = y + jax.nn.sigmoid(_dot(h, wga_ref[:, sl])) * _dot(za, wa_ref[:, sl])
        y_ref[:, sl] = y.astype(BF16)
    o_ref[...] = x + _dot(y_ref[...], wo_ref[...])


def _mixout(x, g, z, conv_p, ob, oc, w_in, wa, wb, wc, wo, l, seq):
    t, d = x.shape
    assert OFF_GATE % d == 0 and seq % TOKEN_TILE == 0 and TOKEN_TILE % CONV_ROWS == 0
    halo_per_tile = TOKEN_TILE // CONV_PAD
    n_halo = t // CONV_PAD

    def tile(n):
        return pl.BlockSpec((TOKEN_TILE, n), lambda i: (i, 0))

    def gate_cols(c):
        blk = OFF_GATE // d + c
        return pl.BlockSpec((None, d, d), lambda i: (l, 0, blk), pipeline_mode=pl.Buffered(1))

    z_before = pl.BlockSpec((CONV_PAD, CONV_CH), lambda i: (jnp.maximum(i * halo_per_tile - 1, 0), 0))
    z_after = pl.BlockSpec((CONV_PAD, CONV_CH), lambda i: (jnp.minimum((i + 1) * halo_per_tile, n_halo - 1), 0))
    return pl.pallas_call(
        functools.partial(_mixout_kernel, tiles_per_seq=seq // TOKEN_TILE),
        grid=(t // TOKEN_TILE,),
        in_specs=[tile(d), _resident((1, d)), z_before, tile(CONV_CH), z_after]
                 + [_resident(p.shape) for p in conv_p]
                 + [tile(WIN_Q), tile(NA_W), gate_cols(0), gate_cols(1), gate_cols(2),
                    _layer(wa, l), _layer(wb, l), _layer(wc, l), _layer(wo, l)],
        out_specs=tile(d),
        out_shape=jax.ShapeDtypeStruct((t, d), F32),
        scratch_shapes=[pltpu.VMEM((TOKEN_TILE, d), BF16)],
        compiler_params=_params("parallel"),
        name="mixout",
    )(x, g, z, z, z, *conv_p, ob, oc, w_in, w_in, w_in, wa, wb, wc, wo)


def _memkv_kernel(m_ref, g_ref, w_ref, k_ref, v_ref):
    h = _rms(m_ref[0], g_ref[...]).astype(BF16)
    k_ref[0] = _dot(h, w_ref[:, :D_MODEL]).astype(BF16)
    v_ref[0] = _dot(h, w_ref[:, D_MODEL:]).astype(BF16)


def _memkv(mem, g, w, l):
    bsz, mlen, d = mem.shape
    blk = pl.BlockSpec((1, mlen, d), lambda b: (b, 0, 0))
    return pl.pallas_call(
        _memkv_kernel,
        grid=(bsz,),
        in_specs=[blk, _resident((1, d)), _layer(w, l)],
        out_specs=[blk, blk],
        out_shape=[jax.ShapeDtypeStruct((bsz, mlen, d), BF16)] * 2,
        compiler_params=_params("parallel"),
        name="memkv",
    )(mem, g, w)


def _cross_kernel(x_ref, g_ref, k_ref, v_ref, wq_ref, wo_ref, o_ref):
    x = x_ref[0]
    h = _rms(x, g_ref[...]).astype(BF16)
    scale = X_HEAD_DIM ** -0.5
    q = (_dot(h, wq_ref[...]) * scale).astype(BF16)
    k = k_ref[0]
    v = v_ref[0]
    outs = []
    for hd in range(X_HEADS):
        hs = slice(hd * X_HEAD_DIM, (hd + 1) * X_HEAD_DIM)
        s = _dot_nt(q[:, hs], k[:, hs])
        m = jnp.max(s, axis=-1, keepdims=True)
        p = jnp.exp(s - m)
        den = jnp.sum(p, axis=-1, keepdims=True)
        outs.append((_dot(p.astype(BF16), v[:, hs]) / den).astype(BF16))
    o = jnp.concatenate(outs, axis=-1)
    o_ref[0] = x + _dot(o, wo_ref[...])


def _cross(x, g, k, v, wq, wo, l):
    bsz, seq, d = x.shape
    mlen = k.shape[1]
    tile = pl.BlockSpec((1, TOKEN_TILE, d), lambda b, i: (b, i, 0))
    kv = pl.BlockSpec((1, mlen, d), lambda b, i: (b, 0, 0))
    return pl.pallas_call(
        _cross_kernel,
        grid=(bsz, seq // TOKEN_TILE),
        in_specs=[tile, _resident((1, d)), kv, kv, _layer(wq, l), _layer(wo, l)],
        out_specs=tile,
        out_shape=jax.ShapeDtypeStruct((bsz, seq, d), F32),
        compiler_params=_params("parallel", "arbitrary"),
        name="cross",
    )(x, g, k, v, wq, wo)


def kernel(x, mem, norm_ffn1, ffn1_w_gate, ffn1_w_up, ffn1_w_down, norm_mix, w_in, conv_dw_w, conv_dw_b, conv_ln_g, conv_ln_b, conv_w_out, win_sink, t5_bias, win_w_out, na_rpb, na_w_out, w_out, norm_cross, norm_mem, cross_w_q, cross_w_kv, cross_w_o, norm_ffn2, ffn2_w_gate, ffn2_w_up, ffn2_w_down, norm_final):
    bsz, seq, d = x.shape
    depth = w_in.shape[0]
    t = bsz * seq
    assert d == D_MODEL and t % TOKEN_TILE == 0 and seq % TOKEN_TILE == 0
    assert seq % (NA_Q_ROWS * GRID_W) == 0 and seq // GRID_W >= NA_ROWS and seq % CONV_ROWS == 0

    def row(v):
        return v.reshape(1, -1).astype(F32)

    def bf(w):
        return w.astype(BF16)

    win_bias = _win_bias(t5_bias * LOG2E)
    win_heads = [h for tl in range(WIN_PAIRS) for h in (tl, tl + WIN_GROUP)]
    gf = row(norm_final)
    ffn1 = (bf(ffn1_w_gate), bf(ffn1_w_up), bf(ffn1_w_down))
    late = (w_in, conv_w_out, win_w_out, na_w_out, w_out, cross_w_q, cross_w_kv, cross_w_o)
    late_ffn = (ffn2_w_gate, ffn2_w_up, ffn2_w_down)

    def flat(w):
        return w.reshape(-1, w.shape[-1]).astype(F32)

    xt = x.reshape(t, d)
    for l in range(depth):
        xt, cast = _ffn(xt, row(norm_ffn1[l]), *ffn1, gf, False, l,
                        side=[flat(w) for w in late] if l == 0 else ())
        if l == 0:
            (w_in_b, conv_w_out_b, win_w_out_b, na_w_out_b, w_out_b, cross_w_q_b, cross_w_kv_b,
             cross_w_o_b) = [c.reshape(w.shape) for c, w in zip(cast, late)]
            w_bq = jnp.concatenate(
                [w_in_b[:, :, OFF_BQ + h * HEAD_DIM:OFF_BQ + (h + 1) * HEAD_DIM] for h in win_heads], axis=2)
            win_w_out_b = jnp.concatenate(
                [win_w_out_b[:, h * HEAD_DIM:(h + 1) * HEAD_DIM] for h in win_heads], axis=1)
            mix_w = (conv_w_out_b, win_w_out_b, na_w_out_b, w_out_b)
            cross_w = (cross_w_q_b, cross_w_o_b)
        (z, bq, bk, bv, cq, ck, cv), cast = _inproj(xt, row(norm_mix[l]), w_in_b, w_bq, l,
                                                    side=[flat(w) for w in late_ffn] if l == 0 else ())
        if l == 0:
            ffn2 = [c.reshape(w.shape) for c, w in zip(cast, late_ffn)]

        def seqs(a):
            return a.reshape(bsz, seq, a.shape[-1])

        conv_p = (conv_dw_w[l].astype(F32), row(conv_dw_b[l]), row(conv_ln_g[l]), row(conv_ln_b[l]))
        ob =_win_attn(seqs(bq), seqs(bk), jnp.swapaxes(seqs(bv), 1, 2), win_sink[l].astype(F32) * LOG2E, win_bias)
        oc = _na_attn(seqs(cq), seqs(ck), seqs(cv), _na_bias(na_rpb[l] * LOG2E))
        xt = _mixout(xt, row(norm_mix[l]), z, conv_p, ob.reshape(t, -1), oc.reshape(t, -1), w_in_b,
                     *mix_w, l, seq)
        mk, mv = _memkv(mem, row(norm_mem[l]), cross_w_kv_b, l)
        xt = _cross(xt.reshape(bsz, seq, d), row(norm_cross[l]), mk, mv, *cross_w, l).reshape(t, d)
        xt, _ = _ffn(xt, row(norm_ffn2[l]), *ffn2, gf, l == depth - 1, l)
    return xt.reshape(bsz, seq, d)
```

```python
import functools

import numpy as np
import jax
import jax.numpy as jnp
from jax import lax
from jax.experimental import pallas as pl
from jax.experimental.pallas import tpu as pltpu

F32 = jnp.float32
BF16 = jnp.bfloat16

D_MODEL = 1024
EPS = 1e-6
LOG2E = 1.4426950408889634
NEG_INF = -1e30
HEAD_DIM = 64
D_FF = 11 * D_MODEL // 4
CONV_CH = D_MODEL // 4
CONV_WIDTH = 31
WIN_HEADS = (D_MODEL // 2) // HEAD_DIM
WIN_KV_HEADS = 2
WIN_GROUP = WIN_HEADS // WIN_KV_HEADS
WINDOW = 128
BLOCK = 128
T5_BUCKETS = 32
T5_MAX_DIST = 128
NA_HEADS = (D_MODEL // 4) // HEAD_DIM
NA_ROWS = 8
NA_COLS = 16
GRID_W = 64
X_HEADS = 4
X_HEAD_DIM = D_MODEL // X_HEADS
N_BRANCH = 3

WIN_Q = WIN_HEADS * HEAD_DIM
WIN_KV = WIN_KV_HEADS * HEAD_DIM
NA_W = NA_HEADS * HEAD_DIM
OFF_CONV = 0
OFF_BQ = OFF_CONV + 2 * CONV_CH
OFF_BK = OFF_BQ + WIN_Q
OFF_BV = OFF_BK + WIN_KV
OFF_CQ = OFF_BV + WIN_KV
OFF_CK = OFF_CQ + NA_W
OFF_CV = OFF_CK + NA_W
OFF_GATE = OFF_CV + NA_W
IN_WIDTH = OFF_GATE + N_BRANCH * D_MODEL

VMEM_LIMIT_BYTES = 56 * 1024 * 1024
TOKEN_TILE = 1024
FF_CHUNK = 256
MIX_CHUNK = 256
CONV_ROWS = 128
CONV_PAD = 16
NA_Q_ROWS = 32
NA_UNROLL = 32


def _params(*sem, flags=None):
    return pltpu.CompilerParams(dimension_semantics=sem, vmem_limit_bytes=VMEM_LIMIT_BYTES, flags=flags)


def _resident(shape):
    zeros = (0,) * len(shape)
    return pl.BlockSpec(shape, lambda *_: zeros, pipeline_mode=pl.Buffered(1))


def _layer(w, l, cols=None):
    shape = w.shape[1:] if cols is None else w.shape[1:-1] + (cols,)
    zeros = (0,) * len(shape)
    return pl.BlockSpec((None,) + shape, lambda *_: (l,) + zeros, pipeline_mode=pl.Buffered(1))


def _side_specs(side, steps):
    packed_rows = 16
    for a in side:
        assert a.ndim == 2 and a.shape[0] % (steps * packed_rows) == 0
    return [pl.BlockSpec((a.shape[0] // steps, a.shape[1]), lambda i: (i, 0)) for a in side]


def _side_shapes(side):
    return [jax.ShapeDtypeStruct(a.shape, BF16) for a in side]


def _cast_side(src_refs, dst_refs):
    for src, dst in zip(src_refs, dst_refs):
        dst[...] = src[...].astype(BF16)


def _rms(x, g):
    ms = jnp.mean(x * x, axis=-1, keepdims=True)
    return x * lax.rsqrt(ms + EPS) * g


def _dot(a, b):
    return jnp.dot(a, b, preferred_element_type=F32)


def _dot_nt(a, b):
    return lax.dot_general(a, b, (((1,), (1,)), ((), ())), preferred_element_type=F32)


def _ffn_kernel(*refs, final_norm, n_side):
    x_ref, g_ref, wg_ref, wu_ref, wd_ref, gf_ref = refs[:6]
    o_ref, act_ref = refs[6 + n_side], refs[-1]
    _cast_side(refs[6:6 + n_side], refs[7 + n_side:7 + 2 * n_side])
    x = x_ref[...]
    h = _rms(x, g_ref[...]).astype(BF16)
    for c in range(D_FF // FF_CHUNK):
        sl = slice(c * FF_CHUNK, (c + 1) * FF_CHUNK)
        a = _dot(h, wg_ref[:, sl])
        b = _dot(h, wu_ref[:, sl])
        act_ref[:, sl] = (a * jax.nn.sigmoid(a) * b).astype(BF16)
    y = x + 0.5 * _dot(act_ref[...], wd_ref[...])
    if final_norm:
        y = _rms(y, gf_ref[...])
    o_ref[...] = y


def _ffn(x, g, wg, wu, wd, gf, final_norm, l, side=()):
    t, d = x.shape
    steps = t // TOKEN_TILE
    tile = pl.BlockSpec((TOKEN_TILE, d), lambda i: (i, 0))
    out = pl.pallas_call(
        functools.partial(_ffn_kernel, final_norm=final_norm, n_side=len(side)),
        grid=(steps,),
        in_specs=[tile, _resident((1, d)), _layer(wg, l), _layer(wu, l), _layer(wd, l), _resident((1, d))]
                 + _side_specs(side, steps),
        out_specs=[tile] + _side_specs(side, steps),
        out_shape=[jax.ShapeDtypeStruct((t, d), F32)] + _side_shapes(side),
        scratch_shapes=[pltpu.VMEM((TOKEN_TILE, D_FF), BF16)],
        compiler_params=_params("parallel"),
        name="ffn",
    )(x, g, wg, wu, wd, gf, *side)
    return out[0], out[1:]


def _inproj_kernel(*refs, n_side):
    x_ref, g_ref, w_ref, wq_ref = refs[:4]
    z_ref, bq_ref, bk_ref, bv_ref, cq_ref, ck_ref, cv_ref = refs[4 + n_side:11 + n_side]
    _cast_side(refs[4:4 + n_side], refs[11 + n_side:])
    h = _rms(x_ref[...], g_ref[...]).astype(BF16)
    scale = HEAD_DIM ** -0.5 * LOG2E
    ua = _dot(h, w_ref[:, OFF_CONV:OFF_BQ])
    z_ref[...] = ua[:, :CONV_CH] * jax.nn.sigmoid(ua[:, CONV_CH:])
    bq_ref[...] = (_dot(h, wq_ref[...]) * scale).astype(BF16)
    bkv = _dot(h, w_ref[:, OFF_BK:OFF_CQ])
    bk_ref[...] = bkv[:, :WIN_KV].astype(BF16)
    bv_ref[...] = bkv[:, WIN_KV:].astype(BF16)
    cq_ref[...] = (_dot(h, w_ref[:, OFF_CQ:OFF_CK]) * scale).astype(BF16)
    ck_ref[...] = _dot(h, w_ref[:, OFF_CK:OFF_CV]).astype(BF16)
    cv_ref[...] = _dot(h, w_ref[:, OFF_CV:OFF_GATE]).astype(BF16)


def _inproj(x, g, w, wq, l, side=()):
    t, d = x.shape
    steps = t // TOKEN_TILE
    widths = (CONV_CH, WIN_Q, WIN_KV, WIN_KV, NA_W, NA_W, NA_W)
    dtypes = (F32,) + (BF16,) * 6

    def tile(n):
        return pl.BlockSpec((TOKEN_TILE, n), lambda i: (i, 0))

    out = pl.pallas_call(
        functools.partial(_inproj_kernel, n_side=len(side)),
        grid=(steps,),
        in_specs=[tile(d), _resident((1, d)), _layer(w, l, cols=OFF_GATE), _layer(wq, l)]
                 + _side_specs(side, steps),
        out_specs=[tile(n) for n in widths] + _side_specs(side, steps),
        out_shape=[jax.ShapeDtypeStruct((t, n), dt) for n, dt in zip(widths, dtypes)] + _side_shapes(side),
        compiler_params=_params("parallel"),
        name="inproj",
    )(x, g, w, wq, *side)
    return out[:len(widths)], out[len(widths):]


CONV_SLAB = CONV_ROWS + 2 * CONV_PAD


def _conv_chunk(slab, w, b, lg, lb):
    half = CONV_WIDTH // 2
    lead = CONV_PAD - half
    n_shift = 8
    span = CONV_SLAB - n_shift
    acc = jnp.zeros((CONV_ROWS, CONV_CH), F32) + b
    for s in range(n_shift):
        shifted = slab if s == 0 else pltpu.roll(slab, CONV_SLAB - s, axis=0)
        for a in range(0, span - CONV_ROWS + 1, n_shift):
            k = a + s - lead
            if 0 <= k < CONV_WIDTH:
                acc = acc + shifted[a:a + CONV_ROWS, :] * w[k:k + 1, :]
    mu = jnp.mean(acc, axis=-1, keepdims=True)
    cen = acc - mu
    var = jnp.mean(cen * cen, axis=-1, keepdims=True)
    y = cen * lax.rsqrt(var + EPS) * lg + lb
    return (y * jax.nn.sigmoid(y)).astype(BF16)


def _t5_buckets(rel):
    half = T5_BUCKETS // 2
    max_exact = half // 2
    ret = (rel > 0).astype(np.int32) * half
    n = np.abs(rel)
    large = max_exact + (np.log(np.maximum(n, 1) / max_exact)
                         / np.log(T5_MAX_DIST / max_exact) * (half - max_exact)).astype(np.int32)
    large = np.minimum(large, half - 1)
    return ret + np.where(n < max_exact, n, large)


WIN_PAIRS = WIN_HEADS // WIN_KV_HEADS
WIN_SPAN = BLOCK + 2 * WINDOW
WIN_QB = 16
WIN_UNROLL = 16
assert WIN_KV_HEADS == 2 and 2 * HEAD_DIM == 128


def _win_bias(t5_table):
    n = WIN_SPAN + BLOCK
    rel = np.arange(n) - (n // 2 - 1)
    line = jnp.where((np.abs(rel) <= WINDOW)[:, None], t5_table[_t5_buckets(rel)].astype(F32), NEG_INF)
    line = jnp.transpose(line)
    skew = jnp.tile(line, (1, BLOCK))[:, :BLOCK * (n - 1)].reshape(WIN_HEADS, BLOCK, n - 1)
    bias = skew[:, :, BLOCK - 1:BLOCK - 1 + WIN_SPAN]
    bias_t = jnp.transpose(bias, (0, 2, 1))
    key = np.arange(WIN_SPAN)[None, :, None]
    first = jnp.where(key < WINDOW, NEG_INF, bias_t)
    last = jnp.where(key >= WINDOW + BLOCK, NEG_INF, bias_t)
    return jnp.stack([first, bias_t, last])


def _win_kernel(sink_ref, q_ref, kp_ref, kc_ref, kn_ref, vp_ref, vc_ref, vn_ref, bias_ref, o_ref,
                k_scr, vt_scr, *, nblk):
    i = pl.program_id(1)
    k_scr[0] = kp_ref[0]
    vt_scr[0] = vp_ref[0]
    for b in range(WIN_QB):
        k_scr[1 + b] = kc_ref[0, b * BLOCK:(b + 1) * BLOCK, :]
        vt_scr[1 + b] = vc_ref[0, :, b * BLOCK:(b + 1) * BLOCK]
    k_scr[WIN_QB + 1] = kn_ref[0]
    vt_scr[WIN_QB + 1] = vn_ref[0]

    def block(jb, carry):
        g = i * WIN_QB + jb
        variant = jnp.where(g == 0, 0, jnp.where(g == nblk - 1, 2, 1))
        rows = pl.ds(pl.multiple_of(jb * BLOCK, BLOCK), BLOCK)
        k = jnp.concatenate([k_scr[jb], k_scr[jb + 1], k_scr[jb + 2]], axis=0)
        vt = jnp.concatenate([vt_scr[jb], vt_scr[jb + 1], vt_scr[jb + 2]], axis=1)
        low = lax.broadcasted_iota(jnp.int32, k.shape, 1) < HEAD_DIM
        k_half = (jnp.where(low, k, 0), jnp.where(low, 0, k))
        ch = lax.broadcasted_iota(jnp.int32, vt.shape, 0)
        den_row = (HEAD_DIM, 0)
        vt_half = (jnp.where(ch < HEAD_DIM, vt, jnp.where(ch == den_row[0], 1, 0).astype(BF16)),
                   jnp.where(ch >= HEAD_DIM, vt, jnp.where(ch == den_row[1], 1, 0).astype(BF16)))
        for t in range(WIN_PAIRS):
            qp = q_ref[0, rows, t * 128:(t + 1) * 128]
            halves = []
            for j in range(WIN_KV_HEADS):
                h = t + j * WIN_GROUP
                sink = sink_ref[h]
                s = _dot_nt(k_half[j], qp) + bias_ref[variant, h]
                m = jnp.maximum(jnp.max(s, axis=0, keepdims=True), sink)
                p = jnp.exp2(s - m).astype(BF16)
                o = _dot(vt_half[j], p)
                den = o[den_row[j]:den_row[j] + 1] + jnp.exp2(sink - m)
                halves.append(o[j * HEAD_DIM:(j + 1) * HEAD_DIM] * (1.0 / den))
            o_ref[0, rows, t * 128:(t + 1) * 128] = jnp.concatenate(halves, axis=0).T.astype(BF16)
        return carry

    lax.fori_loop(0, WIN_QB, block, 0, unroll=WIN_UNROLL)


def _win_attn(q, k, vt, sink, bias):
    bsz, seq, _ = q.shape
    nblk = seq // BLOCK
    assert nblk >= 2 and nblk % WIN_QB == 0

    def edge(shift):
        return lambda b, i: jnp.clip(i * WIN_QB + shift, 0, nblk - 1)

    def k_edge(shift):
        return pl.BlockSpec((1, BLOCK, WIN_KV), lambda b, i: (b, edge(shift)(b, i), 0))

    def v_edge(shift):
        return pl.BlockSpec((1, WIN_KV, BLOCK), lambda b, i: (b, 0, edge(shift)(b, i)))

    qspec = pl.BlockSpec((1, WIN_QB * BLOCK, WIN_Q), lambda b, i: (b, i, 0))
    return pl.pallas_call(
        functools.partial(_win_kernel, nblk=nblk),
        grid=(bsz, nblk // WIN_QB),
        in_specs=[pl.BlockSpec(memory_space=pltpu.SMEM), qspec,
                  k_edge(-1), pl.BlockSpec((1, WIN_QB * BLOCK, WIN_KV), lambda b, i: (b, i, 0)), k_edge(WIN_QB),
                  v_edge(-1), pl.BlockSpec((1, WIN_KV, WIN_QB * BLOCK), lambda b, i: (b, 0, i)), v_edge(WIN_QB),
                  _resident(bias.shape)],
        out_specs=qspec,
        out_shape=jax.ShapeDtypeStruct((bsz, seq, WIN_Q), BF16),
        scratch_shapes=[pltpu.VMEM((WIN_QB + 2, BLOCK, WIN_KV), BF16),
                        pltpu.VMEM((WIN_QB + 2, WIN_KV, BLOCK), BF16)],
        compiler_params=_params("parallel", "parallel"),
        name="win_attn",
    )(sink, q, k, k, k, vt, vt, vt, bias)


def _na_bias(rpb):
    c = np.arange(GRID_W)
    col_start = np.clip(c - NA_COLS // 2, 0, GRID_W - NA_COLS)
    col = np.arange(GRID_W)
    inside = (col[None, :] >= col_start[:, None]) & (col[None, :] < col_start[:, None] + NA_COLS)
    b = rpb.astype(F32)
    n = 2 * GRID_W
    lead = GRID_W - NA_COLS
    line = jnp.pad(b, ((0, 0),) * 2 + ((lead, n - lead - (2 * NA_COLS - 1)),))
    skew = jnp.tile(line, (1, 1, GRID_W))[..., :GRID_W * (n - 1)]
    skew = skew.reshape(b.shape[:2] + (GRID_W, n - 1))[..., GRID_W - 1:2 * GRID_W - 1]
    b = jnp.where(inside[None, None], skew, NEG_INF)
    b = b.reshape((NA_HEADS // 2, 2) + b.shape[1:])
    b = jnp.transpose(b, (0, 2, 4, 1, 3))
    return b.reshape(NA_HEADS // 2, (2 * NA_ROWS - 1) * GRID_W, 2 * GRID_W)


def _na_kernel(q_ref, k_ref, v_ref, bias_ref, o_ref, *, rows):
    j = pl.program_id(1)
    win = NA_ROWS * GRID_W
    low = lax.broadcasted_iota(jnp.int32, (GRID_W, 2 * HEAD_DIM), 1) < HEAD_DIM

    def row(t, carry):
        r = j * NA_Q_ROWS + t
        rs = jnp.clip(r - NA_ROWS // 2, 0, rows - NA_ROWS)
        keys = pl.ds(pl.multiple_of(rs * GRID_W, GRID_W), win)
        qrows = pl.ds(pl.multiple_of(t * GRID_W, GRID_W), GRID_W)
        brows = pl.ds(pl.multiple_of((NA_ROWS - 1 - (r - rs)) * GRID_W, GRID_W), win)
        for tile in range(NA_HEADS // 2):
            lanes = slice(tile * 2 * HEAD_DIM, (tile + 1) * 2 * HEAD_DIM)
            q = q_ref[0, qrows, lanes]
            q_bd = jnp.concatenate([jnp.where(low, q, 0), jnp.where(low, 0, q)], axis=0)
            s = _dot_nt(k_ref[0, keys, lanes], q_bd) + bias_ref[tile, brows, :]
            m = jnp.max(s, axis=0, keepdims=True)
            p = jnp.exp2(s - m)
            den = jnp.sum(p, axis=0, keepdims=True)
            o = lax.dot_general(v_ref[0, keys, lanes], p.astype(BF16), (((0,), (0,)), ((), ())),
                                preferred_element_type=F32)
            ot = (o * (1.0 / den)).T
            o_ref[0, qrows, lanes] = jnp.where(low, ot[:GRID_W], ot[GRID_W:]).astype(BF16)
        return carry

    lax.fori_loop(0, NA_Q_ROWS, row, 0, unroll=NA_UNROLL)


def _na_attn(q, k, v, bias):
    bsz, seq, _ = q.shape
    rows = seq // GRID_W
    qspec = pl.BlockSpec((1, NA_Q_ROWS * GRID_W, NA_W), lambda b, j: (b, j, 0))
    kvspec = pl.BlockSpec((1, seq, NA_W), lambda b, j: (b, 0, 0))
    return pl.pallas_call(
        functools.partial(_na_kernel, rows=rows),
        grid=(bsz, rows // NA_Q_ROWS),
        in_specs=[qspec, kvspec, kvspec, _resident(bias.shape)],
        out_specs=qspec,
        out_shape=jax.ShapeDtypeStruct((bsz, seq, NA_W), BF16),
        compiler_params=_params("parallel", "arbitrary"),
        name="na_attn",
    )(q, k, v, bias)


def _mixout_kernel(x_ref, g_ref, zprev_ref, z_ref, znext_ref, cw_ref, cb_ref, clg_ref, clb_ref,
                   ob_ref, oc_ref, wga_ref, wgb_ref, wgc_ref, wa_ref, wb_ref, wc_ref, wo_ref,
                   o_ref, y_ref, *, tiles_per_seq):
    x = x_ref[...]
    h = _rms(x, g_ref[...]).astype(BF16)
    chunks = [slice(c * MIX_CHUNK, (c + 1) * MIX_CHUNK) for c in range(D_MODEL // MIX_CHUNK)]

    def gated(wg_ref, b_ref, w_ref, sl):
        return jax.nn.sigmoid(_dot(h, wg_ref[:, sl])) * _dot(b_ref[...], w_ref[:, sl])

    pos = pl.program_id(0) % tiles_per_seq
    before = jnp.where(pos != 0, zprev_ref[...], 0.0)
    after = jnp.where(pos != tiles_per_seq - 1, znext_ref[...], 0.0)
    n_chunks = TOKEN_TILE // CONV_ROWS
    conv_args = (cw_ref[...], cb_ref[...], clg_ref[...], clb_ref[...])

    def conv(c):
        lo, hi = c * CONV_ROWS - CONV_PAD, (c + 1) * CONV_ROWS + CONV_PAD
        parts = ([before] if c == 0 else []) + [z_ref[max(lo, 0):min(hi, TOKEN_TILE), :]] \
            + ([after] if c == n_chunks - 1 else [])
        slab = parts[0] if len(parts) == 1 else jnp.concatenate(parts, axis=0)
        return _conv_chunk(slab, *conv_args)

    za = jnp.concatenate([conv(c) for c in range(n_chunks)], axis=0)
    for sl in chunks:
        y = gated(wgb_ref, ob_ref, wb_ref, sl) + gated(wgc_ref, oc_ref, wc_ref, sl)
        y = y + jax.nn.sigmoid(_dot(h, wga_ref[:, sl])) * _dot(za, wa_ref[:, sl])
        y_ref[:, sl] = y.astype(BF16)
    o_ref[...] = x + _dot(y_ref[...], wo_ref[...])


def _mixout(x, g, z, conv_p, ob, oc, w_in, wa, wb, wc, wo, l, seq):
    t, d = x.shape
    assert OFF_GATE % d == 0 and seq % TOKEN_TILE == 0 and TOKEN_TILE % CONV_ROWS == 0
    halo_per_tile = TOKEN_TILE // CONV_PAD
    n_halo = t // CONV_PAD

    def tile(n):
        return pl.BlockSpec((TOKEN_TILE, n), lambda i: (i, 0))

    def gate_cols(c):
        blk = OFF_GATE // d + c
        return pl.BlockSpec((None, d, d), lambda i: (l, 0, blk), pipeline_mode=pl.Buffered(1))

    z_before = pl.BlockSpec((CONV_PAD, CONV_CH), lambda i: (jnp.maximum(i * halo_per_tile - 1, 0), 0))
    z_after = pl.BlockSpec((CONV_PAD, CONV_CH), lambda i: (jnp.minimum((i + 1) * halo_per_tile, n_halo - 1), 0))
    return pl.pallas_call(
        functools.partial(_mixout_kernel, tiles_per_seq=seq // TOKEN_TILE),
        grid=(t // TOKEN_TILE,),
        in_specs=[tile(d), _resident((1, d)), z_before, tile(CONV_CH), z_after]
                 + [_resident(p.shape) for p in conv_p]
                 + [tile(WIN_Q), tile(NA_W), gate_cols(0), gate_cols(1), gate_cols(2),
                    _layer(wa, l), _layer(wb, l), _layer(wc, l), _layer(wo, l)],
        out_specs=tile(d),
        out_shape=jax.ShapeDtypeStruct((t, d), F32),
        scratch_shapes=[pltpu.VMEM((TOKEN_TILE, d), BF16)],
        compiler_params=_params("parallel"),
        name="mixout",
    )(x, g, z, z, z, *conv_p, ob, oc, w_in, w_in, w_in, wa, wb, wc, wo)


def _memkv_kernel(m_ref, g_ref, w_ref, k_ref, v_ref):
    h = _rms(m_ref[0], g_ref[...]).astype(BF16)
    k_ref[0] = _dot(h, w_ref[:, :D_MODEL]).astype(BF16)
    v_ref[0] = _dot(h, w_ref[:, D_MODEL:]).astype(BF16)


def _memkv(mem, g, w, l):
    bsz, mlen, d = mem.shape
    blk = pl.BlockSpec((1, mlen, d), lambda b: (b, 0, 0))
    return pl.pallas_call(
        _memkv_kernel,
        grid=(bsz,),
        in_specs=[blk, _resident((1, d)), _layer(w, l)],
        out_specs=[blk, blk],
        out_shape=[jax.ShapeDtypeStruct((bsz, mlen, d), BF16)] * 2,
        compiler_params=_params("parallel"),
        name="memkv",
    )(mem, g, w)


def _cross_kernel(x_ref, g_ref, k_ref, v_ref, wq_ref, wo_ref, o_ref):
    x = x_ref[0]
    h = _rms(x, g_ref[...]).astype(BF16)
    scale = X_HEAD_DIM ** -0.5
    q = (_dot(h, wq_ref[...]) * scale).astype(BF16)
    k = k_ref[0]
    v = v_ref[0]
    outs = []
    for hd in range(X_HEADS):
        hs = slice(hd * X_HEAD_DIM, (hd + 1) * X_HEAD_DIM)
        s = _dot_nt(q[:, hs], k[:, hs])
        m = jnp.max(s, axis=-1, keepdims=True)
        p = jnp.exp(s - m)
        den = jnp.sum(p, axis=-1, keepdims=True)
        outs.append((_dot(p.astype(BF16), v[:, hs]) / den).astype(BF16))
    o = jnp.concatenate(outs, axis=-1)
    o_ref[0] = x + _dot(o, wo_ref[...])


def _cross(x, g, k, v, wq, wo, l):
    bsz, seq, d = x.shape
    mlen = k.shape[1]
    tile = pl.BlockSpec((1, TOKEN_TILE, d), lambda b, i: (b, i, 0))
    kv = pl.BlockSpec((1, mlen, d), lambda b, i: (b, 0, 0))
    return pl.pallas_call(
        _cross_kernel,
        grid=(bsz, seq // TOKEN_TILE),
        in_specs=[tile, _resident((1, d)), kv, kv, _layer(wq, l), _layer(wo, l)],
        out_specs=tile,
        out_shape=jax.ShapeDtypeStruct((bsz, seq, d), F32),
        compiler_params=_params("parallel", "arbitrary"),
        name="cross",
    )(x, g, k, v, wq, wo)


def kernel(x, mem, norm_ffn1, ffn1_w_gate, ffn1_w_up, ffn1_w_down, norm_mix, w_in, conv_dw_w, conv_dw_b, conv_ln_g, conv_ln_b, conv_w_out, win_sink, t5_bias, win_w_out, na_rpb, na_w_out, w_out, norm_cross, norm_mem, cross_w_q, cross_w_kv, cross_w_o, norm_ffn2, ffn2_w_gate, ffn2_w_up, ffn2_w_down, norm_final):
    bsz, seq, d = x.shape
    depth = w_in.shape[0]
    t = bsz * seq
    assert d == D_MODEL and t % TOKEN_TILE == 0 and seq % TOKEN_TILE == 0
    assert seq % (NA_Q_ROWS * GRID_W) == 0 and seq // GRID_W >= NA_ROWS and seq % CONV_ROWS == 0

    def row(v):
        return v.reshape(1, -1).astype(F32)

    def bf(w):
        return w.astype(BF16)

    win_bias = _win_bias(t5_bias * LOG2E)
    win_heads = [h for tl in range(WIN_PAIRS) for h in (tl, tl + WIN_GROUP)]
    gf = row(norm_final)
    ffn1 = (bf(ffn1_w_gate), bf(ffn1_w_up), bf(ffn1_w_down))
    late = (w_in, conv_w_out, win_w_out, na_w_out, w_out, cross_w_q, cross_w_kv, cross_w_o)
    late_ffn = (ffn2_w_gate, ffn2_w_up, ffn2_w_down)

    def flat(w):
        return w.reshape(-1, w.shape[-1]).astype(F32)

    xt = x.reshape(t, d)
    for l in range(depth):
        xt, cast = _ffn(xt, row(norm_ffn1[l]), *ffn1, gf, False, l,
                        side=[flat(w) for w in late] if l == 0 else ())
        if l == 0:
            (w_in_b, conv_w_out_b, win_w_out_b, na_w_out_b, w_out_b, cross_w_q_b, cross_w_kv_b,
             cross_w_o_b) = [c.reshape(w.shape) for c, w in zip(cast, late)]
            w_bq = jnp.concatenate(
                [w_in_b[:, :, OFF_BQ + h * HEAD_DIM:OFF_BQ + (h + 1) * HEAD_DIM] for h in win_heads], axis=2)
            win_w_out_b = jnp.concatenate(
                [win_w_out_b[:, h * HEAD_DIM:(h + 1) * HEAD_DIM] for h in win_heads], axis=1)
            mix_w = (conv_w_out_b, win_w_out_b, na_w_out_b, w_out_b)
            cross_w = (cross_w_q_b, cross_w_o_b)
        (z, bq, bk, bv, cq, ck, cv), cast = _inproj(xt, row(norm_mix[l]), w_in_b, w_bq, l,
                                                    side=[flat(w) for w in late_ffn] if l == 0 else ())
        if l == 0:
            ffn2 = [c.reshape(w.shape) for c, w in zip(cast, late_ffn)]

        def seqs(a):
            return a.reshape(bsz, seq, a.shape[-1])

        conv_p = (conv_dw_w[l].astype(F32), row(conv_dw_b[l]), row(conv_ln_g[l]), row(conv_ln_b[l]))
        ob = _win_attn(seqs(bq), seqs(bk), jnp.swapaxes(seqs(bv), 1, 2), win_sink[l].astype(F32) * LOG2E, win_bias)
        oc = _na_attn(seqs(cq), seqs(ck), seqs(cv), _na_bias(na_rpb[l] * LOG2E))
        xt = _mixout(xt, row(norm_mix[l]), z, conv_p, ob.reshape(t, -1), oc.reshape(t, -1), w_in_b,
                     *mix_w, l, seq)
        mk, mv = _memkv(mem, row(norm_mem[l]), cross_w_kv_b, l)
        xt = _cross(xt.reshape(bsz, seq, d), row(norm_cross[l]), mk, mv, *cross_w, l).reshape(t, d)
        xt, _ = _ffn(xt, row(norm_ffn2[l]), *ffn2, gf, l == depth - 1, l)
    return xt.reshape(bsz, seq, d)
```

```python
import functools

import numpy as np
import jax
import jax.numpy as jnp
from jax import lax
from jax.experimental import pallas as pl
from jax.experimental.pallas import tpu as pltpu

F32 = jnp.float32
BF16 = jnp.bfloat16

D_MODEL = 1024
EPS = 1e-6
LOG2E = 1.4426950408889634
NEG_INF = -1e30
HEAD_DIM = 64
D_FF = 11 * D_MODEL // 4
CONV_CH = D_MODEL // 4
CONV_WIDTH = 31
WIN_HEADS = (D_MODEL // 2) // HEAD_DIM
WIN_KV_HEADS = 2
WIN_GROUP = WIN_HEADS // WIN_KV_HEADS
WINDOW = 128
BLOCK = 128
T5_BUCKETS = 32
T5_MAX_DIST = 128
NA_HEADS = (D_MODEL // 4) // HEAD_DIM
NA_ROWS = 8
NA_COLS = 16
GRID_W = 64
X_HEADS = 4
X_HEAD_DIM = D_MODEL // X_HEADS
N_BRANCH = 3

WIN_Q = WIN_HEADS * HEAD_DIM
WIN_KV = WIN_KV_HEADS * HEAD_DIM
NA_W = NA_HEADS * HEAD_DIM
OFF_CONV = 0
OFF_BQ = OFF_CONV + 2 * CONV_CH
OFF_BK = OFF_BQ + WIN_Q
OFF_BV = OFF_BK + WIN_KV
OFF_CQ = OFF_BV + WIN_KV
OFF_CK = OFF_CQ + NA_W
OFF_CV = OFF_CK + NA_W
OFF_GATE = OFF_CV + NA_W
IN_WIDTH = OFF_GATE + N_BRANCH * D_MODEL

VMEM_LIMIT_BYTES = 56 * 1024 * 1024
TOKEN_TILE = 1024
FF_CHUNK = 256
MIX_CHUNK = 256
CONV_ROWS = 128
CONV_PAD = 16
NA_Q_ROWS = 32
NA_UNROLL = 32


def _params(*sem, flags=None):
    return pltpu.CompilerParams(dimension_semantics=sem, vmem_limit_bytes=VMEM_LIMIT_BYTES, flags=flags)


def _resident(shape):
    zeros = (0,) * len(shape)
    return pl.BlockSpec(shape, lambda *_: zeros, pipeline_mode=pl.Buffered(1))


def _layer(w, l, cols=None):
    shape = w.shape[1:] if cols is None else w.shape[1:-1] + (cols,)
    zeros = (0,) * len(shape)
    return pl.BlockSpec((None,) + shape, lambda *_: (l,) + zeros, pipeline_mode=pl.Buffered(1))


def _side_specs(side, steps):
    packed_rows = 16
    for a in side:
        assert a.ndim == 2 and a.shape[0] % (steps * packed_rows) == 0
    return [pl.BlockSpec((a.shape[0] // steps, a.shape[1]), lambda i: (i, 0)) for a in side]


def _side_shapes(side):
    return [jax.ShapeDtypeStruct(a.shape, BF16) for a in side]


def _cast_side(src_refs, dst_refs):
    for src, dst in zip(src_refs, dst_refs):
        dst[...] = src[...].astype(BF16)


def _rms(x, g):
    ms = jnp.mean(x * x, axis=-1, keepdims=True)
    return x * lax.rsqrt(ms + EPS) * g


def _dot(a, b):
    return jnp.dot(a, b, preferred_element_type=F32)


def _dot_nt(a, b):
    return lax.dot_general(a, b, (((1,), (1,)), ((), ())), preferred_element_type=F32)


def _ffn_kernel(*refs, final_norm, n_side):
    x_ref, g_ref, wg_ref, wu_ref, wd_ref, gf_ref = refs[:6]
    o_ref, act_ref = refs[6 + n_side], refs[-1]
    _cast_side(refs[6:6 + n_side], refs[7 + n_side:7 + 2 * n_side])
    x = x_ref[...]
    h = _rms(x, g_ref[...]).astype(BF16)
    for c in range(D_FF // FF_CHUNK):
        sl = slice(c * FF_CHUNK, (c + 1) * FF_CHUNK)
        a = _dot(h, wg_ref[:, sl])
        b = _dot(h, wu_ref[:, sl])
        act_ref[:, sl] = (a * jax.nn.sigmoid(a) * b).astype(BF16)
    y = x + 0.5 * _dot(act_ref[...], wd_ref[...])
    if final_norm:
        y = _rms(y, gf_ref[...])
    o_ref[...] = y


def _ffn(x, g, wg, wu, wd, gf, final_norm, l, side=()):
    t, d = x.shape
    steps = t // TOKEN_TILE
    tile = pl.BlockSpec((TOKEN_TILE, d), lambda i: (i, 0))
    out = pl.pallas_call(
        functools.partial(_ffn_kernel, final_norm=final_norm, n_side=len(side)),
        grid=(steps,),
        in_specs=[tile, _resident((1, d)), _layer(wg, l), _layer(wu, l), _layer(wd, l), _resident((1, d))]
                 + _side_specs(side, steps),
        out_specs=[tile] + _side_specs(side, steps),
        out_shape=[jax.ShapeDtypeStruct((t, d), F32)] + _side_shapes(side),
        scratch_shapes=[pltpu.VMEM((TOKEN_TILE, D_FF), BF16)],
        compiler_params=_params("parallel"),
        name="ffn",
    )(x, g, wg, wu, wd, gf, *side)
    return out[0], out[1:]


def _inproj_kernel(*refs, n_side):
    x_ref, g_ref, w_ref, wq_ref = refs[:4]
    z_ref, bq_ref, bk_ref, bv_ref, cq_ref, ck_ref, cv_ref = refs[4 + n_side:11 + n_side]
    _cast_side(refs[4:4 + n_side], refs[11 + n_side:])
    h = _rms(x_ref[...], g_ref[...]).astype(BF16)
    scale = HEAD_DIM ** -0.5 * LOG2E
    ua = _dot(h, w_ref[:, OFF_CONV:OFF_BQ])
    z_ref[...] = ua[:, :CONV_CH] * jax.nn.sigmoid(ua[:, CONV_CH:])
    bq_ref[...] = (_dot(h, wq_ref[...]) * scale).astype(BF16)
    bkv = _dot(h, w_ref[:, OFF_BK:OFF_CQ])
    bk_ref[...] = bkv[:, :WIN_KV].astype(BF16)
    bv_ref[...] = bkv[:, WIN_KV:].astype(BF16)
    cq_ref[...] = (_dot(h, w_ref[:, OFF_CQ:OFF_CK]) * scale).astype(BF16)
    ck_ref[...] = _dot(h, w_ref[:, OFF_CK:OFF_CV]).astype(BF16)
    cv_ref[...] = _dot(h, w_ref[:, OFF_CV:OFF_GATE]).astype(BF16)


def _inproj(x, g, w, wq, l, side=()):
    t, d = x.shape
    steps = t // TOKEN_TILE
    widths = (CONV_CH, WIN_Q, WIN_KV, WIN_KV, NA_W, NA_W, NA_W)
    dtypes = (F32,) + (BF16,) * 6

    def tile(n):
        return pl.BlockSpec((TOKEN_TILE, n), lambda i: (i, 0))

    out = pl.pallas_call(
        functools.partial(_inproj_kernel, n_side=len(side)),
        grid=(steps,),
        in_specs=[tile(d), _resident((1, d)), _layer(w, l, cols=OFF_GATE), _layer(wq, l)]
                 + _side_specs(side, steps),
        out_specs=[tile(n) for n in widths] + _side_specs(side, steps),
        out_shape=[jax.ShapeDtypeStruct((t, n), dt) for n, dt in zip(widths, dtypes)] + _side_shapes(side),
        compiler_params=_params("parallel"),
        name="inproj",
    )(x, g, w, wq, *side)
    return out[:len(widths)], out[len(widths):]


CONV_SLAB = CONV_ROWS + 2 * CONV_PAD


def _conv_chunk(slab, w, b, lg, lb):
    half = CONV_WIDTH // 2
    lead = CONV_PAD - half
    n_shift = 8
    span = CONV_SLAB - n_shift
    acc = jnp.zeros((CONV_ROWS, CONV_CH), F32) + b
    for s in range(n_shift):
        shifted = slab if s == 0 else pltpu.roll(slab, CONV_SLAB - s, axis=0)
        for a in range(0, span - CONV_ROWS + 1, n_shift):
            k = a + s - lead
            if 0 <= k < CONV_WIDTH:
                acc = acc + shifted[a:a + CONV_ROWS, :] * w[k:k + 1, :]
    mu = jnp.mean(acc, axis=-1, keepdims=True)
    cen = acc - mu
    var = jnp.mean(cen * cen, axis=-1, keepdims=True)
    y = cen * lax.rsqrt(var + EPS) * lg + lb
    return (y * jax.nn.sigmoid(y)).astype(BF16)


def _t5_buckets(rel):
    half = T5_BUCKETS // 2
    max_exact = half // 2
    ret = (rel > 0).astype(np.int32) * half
    n = np.abs(rel)
    large = max_exact + (np.log(np.maximum(n, 1) / max_exact)
                         / np.log(T5_MAX_DIST / max_exact) * (half - max_exact)).astype(np.int32)
    large = np.minimum(large, half - 1)
    return ret + np.where(n < max_exact, n, large)


WIN_PAIRS = WIN_HEADS // WIN_KV_HEADS
WIN_SPAN = BLOCK + 2 * WINDOW
WIN_QB = 16
WIN_UNROLL = 16
assert WIN_KV_HEADS == 2 and 2 * HEAD_DIM == 128


def _win_bias(t5_table):
    n = WIN_SPAN + BLOCK
    rel = np.arange(n) - (n // 2 - 1)
    line = jnp.where((np.abs(rel) <= WINDOW)[:, None], t5_table[_t5_buckets(rel)].astype(F32), NEG_INF)
    line = jnp.transpose(line)
    skew = jnp.tile(line, (1, BLOCK))[:, :BLOCK * (n - 1)].reshape(WIN_HEADS, BLOCK, n - 1)
    bias = skew[:, :, BLOCK - 1:BLOCK - 1 + WIN_SPAN]
    bias_t = jnp.transpose(bias, (0, 2, 1))
    key = np.arange(WIN_SPAN)[None, :, None]
    first = jnp.where(key < WINDOW, NEG_INF, bias_t)
    last = jnp.where(key >= WINDOW + BLOCK, NEG_INF, bias_t)
    return jnp.stack([first, bias_t, last])


def _win_kernel(sink_ref, q_ref, kp_ref, kc_ref, kn_ref, vp_ref, vc_ref, vn_ref, bias_ref, o_ref,
                k_scr, vt_scr, *, nblk):
    i = pl.program_id(1)
    k_scr[0] = kp_ref[0]
    vt_scr[0] = vp_ref[0]
    for b in range(WIN_QB):
        k_scr[1 + b] = kc_ref[0, b * BLOCK:(b + 1) * BLOCK, :]
        vt_scr[1 + b] = vc_ref[0, :, b * BLOCK:(b + 1) * BLOCK]
    k_scr[WIN_QB + 1] = kn_ref[0]
    vt_scr[WIN_QB + 1] = vn_ref[0]

    def block(jb, carry):
        g = i * WIN_QB + jb
        variant = jnp.where(g == 0, 0, jnp.where(g == nblk - 1, 2, 1))
        rows = pl.ds(pl.multiple_of(jb * BLOCK, BLOCK), BLOCK)
        k = jnp.concatenate([k_scr[jb], k_scr[jb + 1], k_scr[jb + 2]], axis=0)
        vt = jnp.concatenate([vt_scr[jb], vt_scr[jb + 1], vt_scr[jb + 2]], axis=1)
        low = lax.broadcasted_iota(jnp.int32, k.shape, 1) < HEAD_DIM
        k_half = (jnp.where(low, k, 0), jnp.where(low, 0, k))
        ch = lax.broadcasted_iota(jnp.int32, vt.shape, 0)
        den_row = (HEAD_DIM, 0)
        vt_half = (jnp.where(ch < HEAD_DIM, vt, jnp.where(ch == den_row[0], 1, 0).astype(BF16)),
                   jnp.where(ch >= HEAD_DIM, vt, jnp.where(ch == den_row[1], 1, 0).astype(BF16)))
        for t in range(WIN_PAIRS):
            qp = q_ref[0, rows, t * 128:(t + 1) * 128]
            halves = []
            for j in range(WIN_KV_HEADS):
                h = t + j * WIN_GROUP
                sink = sink_ref[h]
                s = _dot_nt(k_half[j], qp) + bias_ref[variant, h]
                m = jnp.maximum(jnp.max(s, axis=0, keepdims=True), sink)
                p = jnp.exp2(s - m).astype(BF16)
                o = _dot(vt_half[j], p)
                den = o[den_row[j]:den_row[j] + 1] + jnp.exp2(sink - m)
                halves.append(o[j * HEAD_DIM:(j + 1) * HEAD_DIM] * (1.0 / den))
            o_ref[0, rows, t * 128:(t + 1) * 128] = jnp.concatenate(halves, axis=0).T.astype(BF16)
        return carry

    lax.fori_loop(0, WIN_QB, block, 0, unroll=WIN_UNROLL)


def _win_attn(q, k, vt, sink, bias):
    bsz, seq, _ = q.shape
    nblk = seq // BLOCK
    assert nblk >= 2 and nblk % WIN_QB == 0

    def edge(shift):
        return lambda b, i: jnp.clip(i * WIN_QB + shift, 0, nblk - 1)

    def k_edge(shift):
        return pl.BlockSpec((1, BLOCK, WIN_KV), lambda b, i: (b, edge(shift)(b, i), 0))

    def v_edge(shift):
        return pl.BlockSpec((1, WIN_KV, BLOCK), lambda b, i: (b, 0, edge(shift)(b, i)))

    qspec = pl.BlockSpec((1, WIN_QB * BLOCK, WIN_Q), lambda b, i: (b, i, 0))
    return pl.pallas_call(
        functools.partial(_win_kernel, nblk=nblk),
        grid=(bsz, nblk // WIN_QB),
        in_specs=[pl.BlockSpec(memory_space=pltpu.SMEM), qspec,
                  k_edge(-1), pl.BlockSpec((1, WIN_QB * BLOCK, WIN_KV), lambda b, i: (b, i, 0)), k_edge(WIN_QB),
                  v_edge(-1), pl.BlockSpec((1, WIN_KV, WIN_QB * BLOCK), lambda b, i: (b, 0, i)), v_edge(WIN_QB),
                  _resident(bias.shape)],
        out_specs=qspec,
        out_shape=jax.ShapeDtypeStruct((bsz, seq, WIN_Q), BF16),
        scratch_shapes=[pltpu.VMEM((WIN_QB + 2, BLOCK, WIN_KV), BF16),
                        pltpu.VMEM((WIN_QB + 2, WIN_KV, BLOCK), BF16)],
        compiler_params=_params("parallel", "parallel"),
        name="win_attn",
    )(sink, q, k, k, k, vt, vt, vt, bias)


def _na_bias(rpb):
    c = np.arange(GRID_W)
    col_start = np.clip(c - NA_COLS // 2, 0, GRID_W - NA_COLS)
    col = np.arange(GRID_W)
    inside = (col[None, :] >= col_start[:, None]) & (col[None, :] < col_start[:, None] + NA_COLS)
    b = rpb.astype(F32)
    n = 2 * GRID_W
    lead = GRID_W - NA_COLS
    line = jnp.pad(b, ((0, 0),) * 2 + ((lead, n - lead - (2 * NA_COLS - 1)),))
    skew = jnp.tile(line, (1, 1, GRID_W))[..., :GRID_W * (n - 1)]
    skew = skew.reshape(b.shape[:2] + (GRID_W, n - 1))[..., GRID_W - 1:2 * GRID_W - 1]
    b = jnp.where(inside[None, None], skew, NEG_INF)
    b = b.reshape((NA_HEADS // 2, 2) + b.shape[1:])
    b = jnp.transpose(b, (0, 2, 4, 1, 3))
    return b.reshape(NA_HEADS // 2, (2 * NA_ROWS - 1) * GRID_W, 2 * GRID_W)


def _na_kernel(q_ref, k_ref, v_ref, bias_ref, o_ref, *, rows):
    j = pl.program_id(1)
    win = NA_ROWS * GRID_W
    low = lax.broadcasted_iota(jnp.int32, (GRID_W, 2 * HEAD_DIM), 1) < HEAD_DIM

    def row(t, carry):
        r = j * NA_Q_ROWS + t
        rs = jnp.clip(r - NA_ROWS // 2, 0, rows - NA_ROWS)
        keys = pl.ds(pl.multiple_of(rs * GRID_W, GRID_W), win)
        qrows = pl.ds(pl.multiple_of(t * GRID_W, GRID_W), GRID_W)
        brows = pl.ds(pl.multiple_of((NA_ROWS - 1 - (r - rs)) * GRID_W, GRID_W), win)
        for tile in range(NA_HEADS // 2):
            lanes = slice(tile * 2 * HEAD_DIM, (tile + 1) * 2 * HEAD_DIM)
            q = q_ref[0, qrows, lanes]
            q_bd = jnp.concatenate([jnp.where(low, q, 0), jnp.where(low, 0, q)], axis=0)
            s = _dot_nt(k_ref[0, keys, lanes], q_bd) + bias_ref[tile, brows, :]
            m = jnp.max(s, axis=0, keepdims=True)
            p = jnp.exp2(s - m)
            den = jnp.sum(p, axis=0, keepdims=True)
            o = lax.dot_general(v_ref[0, keys, lanes], p.astype(BF16), (((0,), (0,)), ((), ())),
                                preferred_element_type=F32)
            ot = (o * (1.0 / den)).T
            o_ref[0, qrows, lanes] = jnp.where(low, ot[:GRID_W], ot[GRID_W:]).astype(BF16)
        return carry

    lax.fori_loop(0, NA_Q_ROWS, row, 0, unroll=NA_UNROLL)


def _na_attn(q, k, v, bias):
    bsz, seq, _ = q.shape
    rows = seq // GRID_W
    qspec = pl.BlockSpec((1, NA_Q_ROWS * GRID_W, NA_W), lambda b, j: (b, j, 0))
    kvspec = pl.BlockSpec((1, seq, NA_W), lambda b, j: (b, 0, 0))
    return pl.pallas_call(
        functools.partial(_na_kernel, rows=rows),
        grid=(bsz, rows // NA_Q_ROWS),
        in_specs=[qspec, kvspec, kvspec, _resident(bias.shape)],
        out_specs=qspec,
        out_shape=jax.ShapeDtypeStruct((bsz, seq, NA_W), BF16),
        compiler_params=_params("parallel", "arbitrary"),
        name="na_attn",
    )(q, k, v, bias)


def _mixout_kernel(x_ref, g_ref, zprev_ref, z_ref, znext_ref, cw_ref, cb_ref, clg_ref, clb_ref,
                   ob_ref, oc_ref, wga_ref, wgb_ref, wgc_ref, wa_ref, wb_ref, wc_ref, wo_ref,
                   o_ref, y_ref, *, tiles_per_seq):
    x = x_ref[...]
    h = _rms(x, g_ref[...]).astype(BF16)
    chunks = [slice(c * MIX_CHUNK, (c + 1) * MIX_CHUNK) for c in range(D_MODEL // MIX_CHUNK)]

    def gated(wg_ref, b_ref, w_ref, sl):
        return jax.nn.sigmoid(_dot(h, wg_ref[:, sl])) * _dot(b_ref[...], w_ref[:, sl])

    pos = pl.program_id(0) % tiles_per_seq
    before = jnp.where(pos != 0, zprev_ref[...], 0.0)
    after = jnp.where(pos != tiles_per_seq - 1, znext_ref[...], 0.0)
    n_chunks = TOKEN_TILE // CONV_ROWS
    conv_args = (cw_ref[...], cb_ref[...], clg_ref[...], clb_ref[...])

    def conv(c):
        lo, hi = c * CONV_ROWS - CONV_PAD, (c + 1) * CONV_ROWS + CONV_PAD
        parts = ([before] if c == 0 else []) + [z_ref[max(lo, 0):min(hi, TOKEN_TILE), :]] \
            + ([after] if c == n_chunks - 1 else [])
        slab = parts[0] if len(parts) == 1 else jnp.concatenate(parts, axis=0)
        return _conv_chunk(slab, *conv_args)

    za = jnp.concatenate([conv(c) for c in range(n_chunks)], axis=0)
    for sl in chunks:
        y = gated(wgb_ref, ob_ref, wb_ref, sl) + gated(wgc_ref, oc_ref, wc_ref, sl)
        y = y + jax.nn.sigmoid(_dot(h, wga_ref[:, sl])) * _dot(za, wa_ref[:, sl])
        y_ref[:, sl] = y.astype(BF16)
    o_ref[...] = x + _dot(y_ref[...], wo_ref[...])


def _mixout(x, g, z, conv_p, ob, oc, w_in, wa, wb, wc, wo, l, seq):
    t, d = x.shape
    assert OFF_GATE % d == 0 and seq % TOKEN_TILE == 0 and TOKEN_TILE % CONV_ROWS == 0
    halo_per_tile = TOKEN_TILE // CONV_PAD
    n_halo = t // CONV_PAD

    def tile(n):
        return pl.BlockSpec((TOKEN_TILE, n), lambda i: (i, 0))

    def gate_cols(c):
        blk = OFF_GATE // d + c
        return pl.BlockSpec((None, d, d), lambda i: (l, 0, blk), pipeline_mode=pl.Buffered(1))

    z_before = pl.BlockSpec((CONV_PAD, CONV_CH), lambda i: (jnp.maximum(i * halo_per_tile - 1, 0), 0))
    z_after = pl.BlockSpec((CONV_PAD, CONV_CH), lambda i: (jnp.minimum((i + 1) * halo_per_tile, n_halo - 1), 0))
    return pl.pallas_call(
        functools.partial(_mixout_kernel, tiles_per_seq=seq // TOKEN_TILE),
        grid=(t // TOKEN_TILE,),
        in_specs=[tile(d), _resident((1, d)), z_before, tile(CONV_CH), z_after]
                 + [_resident(p.shape) for p in conv_p]
                 + [tile(WIN_Q), tile(NA_W), gate_cols(0), gate_cols(1), gate_cols(2),
                    _layer(wa, l), _layer(wb, l), _layer(wc, l), _layer(wo, l)],
        out_specs=tile(d),
        out_shape=jax.ShapeDtypeStruct((t, d), F32),
        scratch_shapes=[pltpu.VMEM((TOKEN_TILE, d), BF16)],
        compiler_params=_params("parallel"),
        name="mixout",
    )(x, g, z, z, z, *conv_p, ob, oc, w_in, w_in, w_in, wa, wb, wc, wo)


def _memkv_kernel(m_ref, g_ref, w_ref, k_ref, v_ref):
    h = _rms(m_ref[0], g_ref[0]).astype(BF16)
    k_ref[0, 0] = _dot(h, w_ref[:, :D_MODEL]).astype(BF16)
    v_ref[0, 0] = _dot(h, w_ref[:, D_MODEL:]).astype(BF16)


def _memkv(mem, g, w):
    bsz, mlen, d = mem.shape
    depth = w.shape[0]
    out = pl.BlockSpec((1, 1, mlen, d), lambda l, b: (l, b, 0, 0))
    return pl.pallas_call(
        _memkv_kernel,
        grid=(depth, bsz),
        in_specs=[pl.BlockSpec((1, mlen, d), lambda l, b: (b, 0, 0)),
                  pl.BlockSpec((1, 1, d), lambda l, b: (l, 0, 0)),
                  pl.BlockSpec((None,) + w.shape[1:], lambda l, b: (l, 0, 0))],
        out_specs=[out, out],
        out_shape=[jax.ShapeDtypeStruct((depth, bsz, mlen, d), BF16)] * 2,
        compiler_params=_params("parallel", "parallel"),
        name="memkv",
    )(mem, g, w)


def _cross_kernel(x_ref, g_ref, k_ref, v_ref, wq_ref, wo_ref, o_ref):
    x = x_ref[0]
    h = _rms(x, g_ref[...]).astype(BF16)
    scale = X_HEAD_DIM ** -0.5
    q = (_dot(h, wq_ref[...]) * scale).astype(BF16)
    k = k_ref[0]
    v = v_ref[0]
    outs = []
    for hd in range(X_HEADS):
        hs = slice(hd * X_HEAD_DIM, (hd + 1) * X_HEAD_DIM)
        s = _dot_nt(q[:, hs], k[:, hs])
        m = jnp.max(s, axis=-1, keepdims=True)
        p = jnp.exp(s - m)
        den = jnp.sum(p, axis=-1, keepdims=True)
        outs.append((_dot(p.astype(BF16), v[:, hs]) / den).astype(BF16))
    o = jnp.concatenate(outs, axis=-1)
    o_ref[0] = x + _dot(o, wo_ref[...])


def _cross(x, g, k, v, wq, wo, l):
    bsz, seq, d = x.shape
    mlen = k.shape[2]
    tile = pl.BlockSpec((1, TOKEN_TILE, d), lambda b, i: (b, i, 0))
    kv = pl.BlockSpec((None, 1, mlen, d), lambda b, i: (l, b, 0, 0))
    return pl.pallas_call(
        _cross_kernel,
        grid=(bsz, seq // TOKEN_TILE),
        in_specs=[tile, _resident((1, d)), kv, kv, _layer(wq, l), _layer(wo, l)],
        out_specs=tile,
        out_shape=jax.ShapeDtypeStruct((bsz, seq, d), F32),
        compiler_params=_params("parallel", "arbitrary"),
        name="cross",
    )(x, g, k, v, wq, wo)


def kernel(x, mem, norm_ffn1, ffn1_w_gate, ffn1_w_up, ffn1_w_down, norm_mix, w_in, conv_dw_w, conv_dw_b, conv_ln_g, conv_ln_b, conv_w_out, win_sink, t5_bias, win_w_out, na_rpb, na_w_out, w_out, norm_cross, norm_mem, cross_w_q, cross_w_kv, cross_w_o, norm_ffn2, ffn2_w_gate, ffn2_w_up, ffn2_w_down, norm_final):
    bsz, seq, d = x.shape
    depth = w_in.shape[0]
    t = bsz * seq
    assert d == D_MODEL and t % TOKEN_TILE == 0 and seq % TOKEN_TILE == 0
    assert seq % (NA_Q_ROWS * GRID_W) == 0 and seq // GRID_W >= NA_ROWS and seq % CONV_ROWS == 0

    def row(v):
        return v.reshape(1, -1).astype(F32)

    def bf(w):
        return w.astype(BF16)

    win_bias = _win_bias(t5_bias * LOG2E)
    win_heads = [h for tl in range(WIN_PAIRS) for h in (tl, tl + WIN_GROUP)]
    gf = row(norm_final)
    ffn1_all = (ffn1_w_gate, ffn1_w_up, ffn1_w_down)
    ffn1_first = [bf(w[:1]) for w in ffn1_all]
    late = (w_in, conv_w_out, win_w_out, na_w_out, w_out, cross_w_q, cross_w_kv, cross_w_o)
    late_ffn = (ffn2_w_gate, ffn2_w_up, ffn2_w_down) + (tuple(w[1:] for w in ffn1_all) if depth > 1 else ())

    def flat(w):
        rows_unit = (t // TOKEN_TILE) * 16
        for cols in (w.shape[-1], w.shape[-2]):
            if (w.size // cols) % rows_unit == 0:
                return w.reshape(-1, cols).astype(F32)
        raise ValueError(w.shape)

    xt = x.reshape(t, d)
    for l in range(depth):
        ffn1, ffn1_layer = (ffn1_first, 0) if l == 0 else (ffn1_rest, l - 1)
        xt, cast = _ffn(xt, row(norm_ffn1[l]), *ffn1, gf, False, ffn1_layer,
                        side=[flat(w) for w in late] if l == 0 else ())
        if l == 0:
            (w_in_b, conv_w_out_b, win_w_out_b, na_w_out_b, w_out_b, cross_w_q_b, cross_w_kv_b,
             cross_w_o_b) = [c.reshape(w.shape) for c, w in zip(cast, late)]
            w_bq = jnp.concatenate(
                [w_in_b[:, :, OFF_BQ + h * HEAD_DIM:OFF_BQ + (h + 1) * HEAD_DIM] for h in win_heads], axis=2)
            win_w_out_b = jnp.concatenate(
                [win_w_out_b[:, h * HEAD_DIM:(h + 1) * HEAD_DIM] for h in win_heads], axis=1)
            mix_w = (conv_w_out_b, win_w_out_b, na_w_out_b, w_out_b)
            cross_w = (cross_w_q_b, cross_w_o_b)
            mk, mv = _memkv(mem, norm_mem.reshape(depth, 1, d).astype(F32), cross_w_kv_b)
        (z, bq, bk, bv, cq, ck, cv), cast = _inproj(xt, row(norm_mix[l]), w_in_b, w_bq, l,
                                                    side=[flat(w) for w in late_ffn] if l == 0 else ())
        if l == 0:
            cast = [c.reshape(w.shape) for c, w in zip(cast, late_ffn)]
            ffn2, ffn1_rest = cast[:3], cast[3:]

        def seqs(a):
            return a.reshape(bsz, seq, a.shape[-1])

        conv_p = (conv_dw_w[l].astype(F32), row(conv_dw_b[l]), row(conv_ln_g[l]), row(conv_ln_b[l]))
        ob = _win_attn(seqs(bq), seqs(bk), jnp.swapaxes(seqs(bv), 1, 2), win_sink[l].astype(F32) * LOG2E, win_bias)
        oc = _na_attn(seqs(cq), seqs(ck), seqs(cv), _na_bias(na_rpb[l] * LOG2E))
        xt = _mixout(xt, row(norm_mix[l]), z, conv_p, ob.reshape(t, -1), oc.reshape(t, -1), w_in_b,
                     *mix_w, l, seq)
        xt = _cross(xt.reshape(bsz, seq, d), row(norm_cross[l]), mk, mv, *cross_w, l).reshape(t, d)
        xt, _ = _ffn(xt, row(norm_ffn2[l]), *ffn2, gf, l == depth - 1, l)
    return xt.reshape(bsz, seq, d)
```

```python
import functools

import numpy as np
import jax
import jax.numpy as jnp
from jax import lax
from jax.experimental import pallas as pl
from jax.experimental.pallas import tpu as pltpu

F32 = jnp.float32
BF16 = jnp.bfloat16

D_MODEL = 1024
EPS = 1e-6
LOG2E = 1.4426950408889634
NEG_INF = -1e30
HEAD_DIM = 64
D_FF = 11 * D_MODEL // 4
CONV_CH = D_MODEL // 4
CONV_WIDTH = 31
WIN_HEADS = (D_MODEL // 2) // HEAD_DIM
WIN_KV_HEADS = 2
WIN_GROUP = WIN_HEADS // WIN_KV_HEADS
WINDOW = 128
BLOCK = 128
T5_BUCKETS = 32
T5_MAX_DIST = 128
NA_HEADS = (D_MODEL // 4) // HEAD_DIM
NA_ROWS = 8
NA_COLS = 16
GRID_W = 64
X_HEADS = 4
X_HEAD_DIM = D_MODEL // X_HEADS
N_BRANCH = 3

WIN_Q = WIN_HEADS * HEAD_DIM
WIN_KV = WIN_KV_HEADS * HEAD_DIM
NA_W = NA_HEADS * HEAD_DIM
OFF_CONV = 0
OFF_BQ = OFF_CONV + 2 * CONV_CH
OFF_BK = OFF_BQ + WIN_Q
OFF_BV = OFF_BK + WIN_KV
OFF_CQ = OFF_BV + WIN_KV
OFF_CK = OFF_CQ + NA_W
OFF_CV = OFF_CK + NA_W
OFF_GATE = OFF_CV + NA_W
IN_WIDTH = OFF_GATE + N_BRANCH * D_MODEL

VMEM_LIMIT_BYTES = 56 * 1024 * 1024
TOKEN_TILE = 1024
FF_CHUNK = 256
MIX_CHUNK = 256
CONV_ROWS = 128
CONV_PAD = 16
NA_Q_ROWS = 32
NA_UNROLL = 32


def _params(*sem, flags=None):
    return pltpu.CompilerParams(dimension_semantics=sem, vmem_limit_bytes=VMEM_LIMIT_BYTES, flags=flags)


def _resident(shape):
    zeros = (0,) * len(shape)
    return pl.BlockSpec(shape, lambda *_: zeros, pipeline_mode=pl.Buffered(1))


def _layer(w, l, cols=None):
    shape = w.shape[1:] if cols is None else w.shape[1:-1] + (cols,)
    zeros = (0,) * len(shape)
    return pl.BlockSpec((None,) + shape, lambda *_: (l,) + zeros, pipeline_mode=pl.Buffered(1))


def _side_specs(side, steps):
    packed_rows = 16
    for a in side:
        assert a.ndim == 2 and a.shape[0] % (steps * packed_rows) == 0
    return [pl.BlockSpec((a.shape[0] // steps, a.shape[1]), lambda i: (i, 0)) for a in side]


def _side_shapes(side):
    return [jax.ShapeDtypeStruct(a.shape, BF16) for a in side]


def _cast_side(src_refs, dst_refs):
    for src, dst in zip(src_refs, dst_refs):
        dst[...] = src[...].astype(BF16)


def _rms(x, g):
    ms = jnp.mean(x * x, axis=-1, keepdims=True)
    return x * lax.rsqrt(ms + EPS) * g


def _dot(a, b):
    return jnp.dot(a, b, preferred_element_type=F32)


def _dot_nt(a, b):
    return lax.dot_general(a, b, (((1,), (1,)), ((), ())), preferred_element_type=F32)


def _ffn_kernel(*refs, final_norm, n_side):
    x_ref, g_ref, wg_ref, wu_ref, wd_ref, gf_ref = refs[:6]
    o_ref, act_ref = refs[6 + n_side], refs[-1]
    _cast_side(refs[6:6 + n_side], refs[7 + n_side:7 + 2 * n_side])
    x = x_ref[...]
    h = _rms(x, g_ref[...]).astype(BF16)
    for c in range(D_FF // FF_CHUNK):
        sl = slice(c * FF_CHUNK, (c + 1) * FF_CHUNK)
        a = _dot(h, wg_ref[:, sl])
        b = _dot(h, wu_ref[:, sl])
        act_ref[:, sl] = (a * jax.nn.sigmoid(a) * b).astype(BF16)
    y = x + 0.5 * _dot(act_ref[...], wd_ref[...])
    if final_norm:
        y = _rms(y, gf_ref[...])
    o_ref[...] = y


def _ffn(x, g, wg, wu, wd, gf, final_norm, l, side=()):
    t, d = x.shape
    steps = t // TOKEN_TILE
    tile = pl.BlockSpec((TOKEN_TILE, d), lambda i: (i, 0))
    out = pl.pallas_call(
        functools.partial(_ffn_kernel, final_norm=final_norm, n_side=len(side)),
        grid=(steps,),
        in_specs=[tile, _resident((1, d)), _layer(wg, l), _layer(wu, l), _layer(wd, l), _resident((1, d))]
                 + _side_specs(side, steps),
        out_specs=[tile] + _side_specs(side, steps),
        out_shape=[jax.ShapeDtypeStruct((t, d), F32)] + _side_shapes(side),
        scratch_shapes=[pltpu.VMEM((TOKEN_TILE, D_FF), BF16)],
        compiler_params=_params("parallel"),
        name="ffn",
    )(x, g, wg, wu, wd, gf, *side)
    return out[0], out[1:]


def _inproj_kernel(*refs, n_side):
    x_ref, g_ref, w_ref, wq_ref = refs[:4]
    z_ref, bq_ref, bk_ref, bv_ref, cq_ref, ck_ref, cv_ref = refs[4 + n_side:11 + n_side]
    _cast_side(refs[4:4 + n_side], refs[11 + n_side:])
    h = _rms(x_ref[...], g_ref[...]).astype(BF16)
    scale = HEAD_DIM ** -0.5 * LOG2E
    ua = _dot(h, w_ref[:, OFF_CONV:OFF_BQ])
    z_ref[...] = ua[:, :CONV_CH] * jax.nn.sigmoid(ua[:, CONV_CH:])
    bq_ref[...] = (_dot(h, wq_ref[...]) * scale).astype(BF16)
    bkv = _dot(h, w_ref[:, OFF_BK:OFF_CQ])
    bk_ref[...] = bkv[:, :WIN_KV].astype(BF16)
    bv_ref[...] = bkv[:, WIN_KV:].astype(BF16)
    cq_ref[...] = (_dot(h, w_ref[:, OFF_CQ:OFF_CK]) * scale).astype(BF16)
    ck_ref[...] = _dot(h, w_ref[:, OFF_CK:OFF_CV]).astype(BF16)
    cv_ref[...] = _dot(h, w_ref[:, OFF_CV:OFF_GATE]).astype(BF16)


def _inproj(x, g, w, wq, l, side=()):
    t, d = x.shape
    steps = t // TOKEN_TILE
    widths = (CONV_CH, WIN_Q, WIN_KV, WIN_KV, NA_W, NA_W, NA_W)
    dtypes = (F32,) + (BF16,) * 6

    def tile(n):
        return pl.BlockSpec((TOKEN_TILE, n), lambda i: (i, 0))

    out = pl.pallas_call(
        functools.partial(_inproj_kernel, n_side=len(side)),
        grid=(steps,),
        in_specs=[tile(d), _resident((1, d)), _layer(w, l, cols=OFF_GATE), _layer(wq, l)]
                 + _side_specs(side, steps),
        out_specs=[tile(n) for n in widths] + _side_specs(side, steps),
        out_shape=[jax.ShapeDtypeStruct((t, n), dt) for n, dt in zip(widths, dtypes)] + _side_shapes(side),
        compiler_params=_params("parallel"),
        name="inproj",
    )(x, g, w, wq, *side)
    return out[:len(widths)], out[len(widths):]


CONV_SLAB = CONV_ROWS + 2 * CONV_PAD


def _conv_chunk(slab, w, b, lg, lb):
    half = CONV_WIDTH // 2
    lead = CONV_PAD - half
    n_shift = 8
    span = CONV_SLAB - n_shift
    acc = jnp.zeros((CONV_ROWS, CONV_CH), F32) + b
    for s in range(n_shift):
        shifted = slab if s == 0 else pltpu.roll(slab, CONV_SLAB - s, axis=0)
        for a in range(0, span - CONV_ROWS + 1, n_shift):
            k = a + s - lead
            if 0 <= k < CONV_WIDTH:
                acc = acc + shifted[a:a + CONV_ROWS, :] * w[k:k + 1, :]
    mu = jnp.mean(acc, axis=-1, keepdims=True)
    cen = acc - mu
    var = jnp.mean(cen * cen, axis=-1, keepdims=True)
    y = cen * lax.rsqrt(var + EPS) * lg + lb
    return (y * jax.nn.sigmoid(y)).astype(BF16)


def _t5_buckets(rel):
    half = T5_BUCKETS // 2
    max_exact = half // 2
    ret = (rel > 0).astype(np.int32) * half
    n = np.abs(rel)
    large = max_exact + (np.log(np.maximum(n, 1) / max_exact)
                         / np.log(T5_MAX_DIST / max_exact) * (half - max_exact)).astype(np.int32)
    large = np.minimum(large, half - 1)
    return ret + np.where(n < max_exact, n, large)


WIN_PAIRS = WIN_HEADS // WIN_KV_HEADS
WIN_SPAN = BLOCK + 2 * WINDOW
WIN_QB = 16
WIN_UNROLL = 16
assert WIN_KV_HEADS == 2 and 2 * HEAD_DIM == 128


def _win_bias(t5_table):
    n = WIN_SPAN + BLOCK
    rel = np.arange(n) - (n // 2 - 1)
    line = jnp.where((np.abs(rel) <= WINDOW)[:, None], t5_table[_t5_buckets(rel)].astype(F32), NEG_INF)
    line = jnp.transpose(line)
    skew = jnp.tile(line, (1, BLOCK))[:, :BLOCK * (n - 1)].reshape(WIN_HEADS, BLOCK, n - 1)
    bias = skew[:, :, BLOCK - 1:BLOCK - 1 + WIN_SPAN]
    bias_t = jnp.transpose(bias, (0, 2, 1))
    key = np.arange(WIN_SPAN)[None, :, None]
    first = jnp.where(key < WINDOW, NEG_INF, bias_t)
    last = jnp.where(key >= WINDOW + BLOCK, NEG_INF, bias_t)
    return jnp.stack([first, bias_t, last])


def _win_kernel(sink_ref, q_ref, kp_ref, kc_ref, kn_ref, vp_ref, vc_ref, vn_ref, bias_ref, o_ref,
                k_scr, vt_scr, *, nblk):
    i = pl.program_id(1)
    k_scr[0] = kp_ref[0]
    vt_scr[0] = vp_ref[0]
    for b in range(WIN_QB):
        k_scr[1 + b] = kc_ref[0, b * BLOCK:(b + 1) * BLOCK, :]
        vt_scr[1 + b] = vc_ref[0, :, b * BLOCK:(b + 1) * BLOCK]
    k_scr[WIN_QB + 1] = kn_ref[0]
    vt_scr[WIN_QB + 1] = vn_ref[0]

    def block(jb, carry):
        g = i * WIN_QB + jb
        variant = jnp.where(g == 0, 0, jnp.where(g == nblk - 1, 2, 1))
        rows = pl.ds(pl.multiple_of(jb * BLOCK, BLOCK), BLOCK)
        k = jnp.concatenate([k_scr[jb], k_scr[jb + 1], k_scr[jb + 2]], axis=0)
        vt = jnp.concatenate([vt_scr[jb], vt_scr[jb + 1], vt_scr[jb + 2]], axis=1)
        low = lax.broadcasted_iota(jnp.int32, k.shape, 1) < HEAD_DIM
        k_half = (jnp.where(low, k, 0), jnp.where(low, 0, k))
        ch = lax.broadcasted_iota(jnp.int32, vt.shape, 0)
        den_row = (HEAD_DIM, 0)
        vt_half = (jnp.where(ch < HEAD_DIM, vt, jnp.where(ch == den_row[0], 1, 0).astype(BF16)),
                   jnp.where(ch >= HEAD_DIM, vt, jnp.where(ch == den_row[1], 1, 0).astype(BF16)))
        for t in range(WIN_PAIRS):
            qp = q_ref[0, rows, t * 128:(t + 1) * 128]
            halves = []
            for j in range(WIN_KV_HEADS):
                h = t + j * WIN_GROUP
                sink = sink_ref[h]
                s = _dot_nt(k_half[j], qp) + bias_ref[variant, h]
                m = jnp.maximum(jnp.max(s, axis=0, keepdims=True), sink)
                p = jnp.exp2(s - m).astype(BF16)
                o = _dot(vt_half[j], p)
                den = o[den_row[j]:den_row[j] + 1] + jnp.exp2(sink - m)
                halves.append(o[j * HEAD_DIM:(j + 1) * HEAD_DIM] * (1.0 / den))
            o_ref[0, rows, t * 128:(t + 1) * 128] = jnp.concatenate(halves, axis=0).T.astype(BF16)
        return carry

    lax.fori_loop(0, WIN_QB, block, 0, unroll=WIN_UNROLL)


def _na_bias(rpb):
    c = np.arange(GRID_W)
    col_start = np.clip(c - NA_COLS // 2, 0, GRID_W - NA_COLS)
    col = np.arange(GRID_W)
    inside = (col[None, :] >= col_start[:, None]) & (col[None, :] < col_start[:, None] + NA_COLS)
    b = rpb.astype(F32)
    n = 2 * GRID_W
    lead = GRID_W - NA_COLS
    line = jnp.pad(b, ((0, 0),) * 2 + ((lead, n - lead - (2 * NA_COLS - 1)),))
    skew = jnp.tile(line, (1, 1, GRID_W))[..., :GRID_W * (n - 1)]
    skew = skew.reshape(b.shape[:2] + (GRID_W, n - 1))[..., GRID_W - 1:2 * GRID_W - 1]
    b = jnp.where(inside[None, None], skew, NEG_INF)
    b = b.reshape((NA_HEADS // 2, 2) + b.shape[1:])
    b = jnp.transpose(b, (0, 2, 4, 1, 3))
    return b.reshape(NA_HEADS // 2, (2 * NA_ROWS - 1) * GRID_W, 2 * GRID_W)


def _na_kernel(q_ref, k_ref, v_ref, bias_ref, o_ref, *, rows):
    j = pl.program_id(1)
    win = NA_ROWS * GRID_W
    low = lax.broadcasted_iota(jnp.int32, (GRID_W, 2 * HEAD_DIM), 1) < HEAD_DIM

    def row(t, carry):
        r = j * NA_Q_ROWS + t
        rs = jnp.clip(r - NA_ROWS // 2, 0, rows - NA_ROWS)
        keys = pl.ds(pl.multiple_of(rs * GRID_W, GRID_W), win)
        qrows = pl.ds(pl.multiple_of(t * GRID_W, GRID_W), GRID_W)
        brows = pl.ds(pl.multiple_of((NA_ROWS - 1 - (r - rs)) * GRID_W, GRID_W), win)
        for tile in range(NA_HEADS // 2):
            lanes = slice(tile * 2 * HEAD_DIM, (tile + 1) * 2 * HEAD_DIM)
            q = q_ref[0, qrows, lanes]
            q_bd = jnp.concatenate([jnp.where(low, q, 0), jnp.where(low, 0, q)], axis=0)
            s = _dot_nt(k_ref[0, keys, lanes], q_bd) + bias_ref[tile, brows, :]
            m = jnp.max(s, axis=0, keepdims=True)
            p = jnp.exp2(s - m)
            den = jnp.sum(p, axis=0, keepdims=True)
            o = lax.dot_general(v_ref[0, keys, lanes], p.astype(BF16), (((0,), (0,)), ((), ())),
                                preferred_element_type=F32)
            ot = (o * (1.0 / den)).T
            o_ref[0, qrows, lanes] = jnp.where(low, ot[:GRID_W], ot[GRID_W:]).astype(BF16)
        return carry

    lax.fori_loop(0, NA_Q_ROWS, row, 0, unroll=NA_UNROLL)


def _attn_kernel(*refs, nblk, rows):
    _win_kernel(*refs[:9], refs[13], refs[15], refs[16], nblk=nblk)
    _na_kernel(*refs[9:13], refs[14], rows=rows)


def _attn(bq, bk, bvt, sink, win_bias, cq, ck, cv, na_bias):
    bsz, seq, _ = bq.shape
    nblk = seq // BLOCK
    rows = seq // GRID_W
    assert nblk >= 2 and nblk % WIN_QB == 0 and WIN_QB * BLOCK == NA_Q_ROWS * GRID_W

    def edge(shift):
        return lambda b, i: jnp.clip(i * WIN_QB + shift, 0, nblk - 1)

    def k_edge(shift):
        return pl.BlockSpec((1, BLOCK, WIN_KV), lambda b, i: (b, edge(shift)(b, i), 0))

    def v_edge(shift):
        return pl.BlockSpec((1, WIN_KV, BLOCK), lambda b, i: (b, 0, edge(shift)(b, i)))

    span = WIN_QB * BLOCK
    qspec = pl.BlockSpec((1, span, WIN_Q), lambda b, i: (b, i, 0))
    cqspec = pl.BlockSpec((1, span, NA_W), lambda b, i: (b, i, 0))
    ckvspec = pl.BlockSpec((1, seq, NA_W), lambda b, i: (b, 0, 0))
    return pl.pallas_call(
        functools.partial(_attn_kernel, nblk=nblk, rows=rows),
        grid=(bsz, nblk // WIN_QB),
        in_specs=[pl.BlockSpec(memory_space=pltpu.SMEM), qspec,
                  k_edge(-1), pl.BlockSpec((1, span, WIN_KV), lambda b, i: (b, i, 0)), k_edge(WIN_QB),
                  v_edge(-1), pl.BlockSpec((1, WIN_KV, span), lambda b, i: (b, 0, i)), v_edge(WIN_QB),
                  _resident(win_bias.shape),
                  cqspec, ckvspec, ckvspec, _resident(na_bias.shape)],
        out_specs=[qspec, cqspec],
        out_shape=[jax.ShapeDtypeStruct((bsz, seq, WIN_Q), BF16), jax.ShapeDtypeStruct((bsz, seq, NA_W), BF16)],
        scratch_shapes=[pltpu.VMEM((WIN_QB + 2, BLOCK, WIN_KV), BF16),
                        pltpu.VMEM((WIN_QB + 2, WIN_KV, BLOCK), BF16)],
        compiler_params=_params("parallel", "arbitrary"),
        name="attn",
    )(sink, bq, bk, bk, bk, bvt, bvt, bvt, win_bias, cq, ck, cv, na_bias)


def _mixout_kernel(x_ref, g_ref, zprev_ref, z_ref, znext_ref, cw_ref, cb_ref, clg_ref, clb_ref,
                   ob_ref, oc_ref, wga_ref, wgb_ref, wgc_ref, wa_ref, wb_ref, wc_ref, wo_ref,
                   o_ref, y_ref, *, tiles_per_seq):
    x = x_ref[...]
    h = _rms(x, g_ref[...]).astype(BF16)
    chunks = [slice(c * MIX_CHUNK, (c + 1) * MIX_CHUNK) for c in range(D_MODEL // MIX_CHUNK)]

    def gated(wg_ref, b_ref, w_ref, sl):
        return jax.nn.sigmoid(_dot(h, wg_ref[:, sl])) * _dot(b_ref[...], w_ref[:, sl])

    pos = pl.program_id(0) % tiles_per_seq
    before = jnp.where(pos != 0, zprev_ref[...], 0.0)
    after = jnp.where(pos != tiles_per_seq - 1, znext_ref[...], 0.0)
    n_chunks = TOKEN_TILE // CONV_ROWS
    conv_args = (cw_ref[...], cb_ref[...], clg_ref[...], clb_ref[...])

    def conv(c):
        lo, hi = c * CONV_ROWS - CONV_PAD, (c + 1) * CONV_ROWS + CONV_PAD
        parts = ([before] if c == 0 else []) + [z_ref[max(lo, 0):min(hi, TOKEN_TILE), :]] \
            + ([after] if c == n_chunks - 1 else [])
        slab = parts[0] if len(parts) == 1 else jnp.concatenate(parts, axis=0)
        return _conv_chunk(slab, *conv_args)

    za = jnp.concatenate([conv(c) for c in range(n_chunks)], axis=0)
    for sl in chunks:
        y = gated(wgb_ref, ob_ref, wb_ref, sl) + gated(wgc_ref, oc_ref, wc_ref, sl)
        y = y + jax.nn.sigmoid(_dot(h, wga_ref[:, sl])) * _dot(za, wa_ref[:, sl])
        y_ref[:, sl] = y.astype(BF16)
    o_ref[...] = x + _dot(y_ref[...], wo_ref[...])


def _mixout(x, g, z, conv_p, ob, oc, w_in, wa, wb, wc, wo, l, seq):
    t, d = x.shape
    assert OFF_GATE % d == 0 and seq % TOKEN_TILE == 0 and TOKEN_TILE % CONV_ROWS == 0
    halo_per_tile = TOKEN_TILE // CONV_PAD
    n_halo = t // CONV_PAD

    def tile(n):
        return pl.BlockSpec((TOKEN_TILE, n), lambda i: (i, 0))

    def gate_cols(c):
        blk = OFF_GATE // d + c
        return pl.BlockSpec((None, d, d), lambda i: (l, 0, blk), pipeline_mode=pl.Buffered(1))

    z_before = pl.BlockSpec((CONV_PAD, CONV_CH), lambda i: (jnp.maximum(i * halo_per_tile - 1, 0), 0))
    z_after = pl.BlockSpec((CONV_PAD, CONV_CH), lambda i: (jnp.minimum((i + 1) * halo_per_tile, n_halo - 1), 0))
    return pl.pallas_call(
        functools.partial(_mixout_kernel, tiles_per_seq=seq // TOKEN_TILE),
        grid=(t // TOKEN_TILE,),
        in_specs=[tile(d), _resident((1, d)), z_before, tile(CONV_CH), z_after]
                 + [_resident(p.shape) for p in conv_p]
                 + [tile(WIN_Q), tile(NA_W), gate_cols(0), gate_cols(1), gate_cols(2),
                    _layer(wa, l), _layer(wb, l), _layer(wc, l), _layer(wo, l)],
        out_specs=tile(d),
        out_shape=jax.ShapeDtypeStruct((t, d), F32),
        scratch_shapes=[pltpu.VMEM((TOKEN_TILE, d), BF16)],
        compiler_params=_params("parallel"),
        name="mixout",
    )(x, g, z, z, z, *conv_p, ob, oc, w_in, w_in, w_in, wa, wb, wc, wo)


def _memkv_kernel(m_ref, g_ref, w_ref, k_ref, v_ref):
    h = _rms(m_ref[0], g_ref[...]).astype(BF16)
    k_ref[0] = _dot(h, w_ref[:, :D_MODEL]).astype(BF16)
    v_ref[0] = _dot(h, w_ref[:, D_MODEL:]).astype(BF16)


def _memkv(mem, g, w, l):
    bsz, mlen, d = mem.shape
    blk = pl.BlockSpec((1, mlen, d), lambda b: (b, 0, 0))
    return pl.pallas_call(
        _memkv_kernel,
        grid=(bsz,),
        in_specs=[blk, _resident((1, d)), _layer(w, l)],
        out_specs=[blk, blk],
        out_shape=[jax.ShapeDtypeStruct((bsz, mlen, d), BF16)] * 2,
        compiler_params=_params("parallel"),
        name="memkv",
    )(mem, g, w)


def _cross_kernel(x_ref, g_ref, k_ref, v_ref, wq_ref, wo_ref, o_ref):
    x = x_ref[0]
    h = _rms(x, g_ref[...]).astype(BF16)
    scale = X_HEAD_DIM ** -0.5
    q = (_dot(h, wq_ref[...]) * scale).astype(BF16)
    k = k_ref[0]
    v = v_ref[0]
    outs = []
    for hd in range(X_HEADS):
        hs = slice(hd * X_HEAD_DIM, (hd + 1) * X_HEAD_DIM)
        s = _dot_nt(q[:, hs], k[:, hs])
        m = jnp.max(s, axis=-1, keepdims=True)
        p = jnp.exp(s - m)
        den = jnp.sum(p, axis=-1, keepdims=True)
        outs.append((_dot(p.astype(BF16), v[:, hs]) / den).astype(BF16))
    o = jnp.concatenate(outs, axis=-1)
    o_ref[0] = x + _dot(o, wo_ref[...])


def _cross(x, g, k, v, wq, wo, l):
    bsz, seq, d = x.shape
    mlen = k.shape[1]
    tile = pl.BlockSpec((1, TOKEN_TILE, d), lambda b, i: (b, i, 0))
    kv = pl.BlockSpec((1, mlen, d), lambda b, i: (b, 0, 0))
    return pl.pallas_call(
        _cross_kernel,
        grid=(bsz, seq // TOKEN_TILE),
        in_specs=[tile, _resident((1, d)), kv, kv, _layer(wq, l), _layer(wo, l)],
        out_specs=tile,
        out_shape=jax.ShapeDtypeStruct((bsz, seq, d), F32),
        compiler_params=_params("parallel", "arbitrary"),
        name="cross",
    )(x, g, k, v, wq, wo)


def kernel(x, mem, norm_ffn1, ffn1_w_gate, ffn1_w_up, ffn1_w_down, norm_mix, w_in, conv_dw_w, conv_dw_b, conv_ln_g, conv_ln_b, conv_w_out, win_sink, t5_bias, win_w_out, na_rpb, na_w_out, w_out, norm_cross, norm_mem, cross_w_q, cross_w_kv, cross_w_o, norm_ffn2, ffn2_w_gate, ffn2_w_up, ffn2_w_down, norm_final):
    bsz, seq, d = x.shape
    depth = w_in.shape[0]
    t = bsz * seq
    assert d == D_MODEL and t % TOKEN_TILE == 0 and seq % TOKEN_TILE == 0
    assert seq % (NA_Q_ROWS * GRID_W) == 0 and seq // GRID_W >= NA_ROWS and seq % CONV_ROWS == 0

    def row(v):
        return v.reshape(1, -1).astype(F32)

    def bf(w):
        return w.astype(BF16)

    win_bias = _win_bias(t5_bias * LOG2E)
    win_heads = [h for tl in range(WIN_PAIRS) for h in (tl, tl + WIN_GROUP)]
    gf = row(norm_final)
    ffn1 = (bf(ffn1_w_gate), bf(ffn1_w_up), bf(ffn1_w_down))
    late = (w_in, conv_w_out, win_w_out, na_w_out, w_out, cross_w_q, cross_w_kv, cross_w_o)
    late_ffn = (ffn2_w_gate, ffn2_w_up, ffn2_w_down)

    def flat(w):
        return w.reshape(-1, w.shape[-1]).astype(F32)

    xt = x.reshape(t, d)
    for l in range(depth):
        xt, cast = _ffn(xt, row(norm_ffn1[l]), *ffn1, gf, False, l,
                        side=[flat(w) for w in late] if l == 0 else ())
        if l == 0:
            (w_in_b, conv_w_out_b, win_w_out_b, na_w_out_b, w_out_b, cross_w_q_b, cross_w_kv_b,
             cross_w_o_b) = [c.reshape(w.shape) for c, w in zip(cast, late)]
            w_bq = jnp.concatenate(
                [w_in_b[:, :, OFF_BQ + h * HEAD_DIM:OFF_BQ + (h + 1) * HEAD_DIM] for h in win_heads], axis=2)
            win_w_out_b = jnp.concatenate(
                [win_w_out_b[:, h * HEAD_DIM:(h + 1) * HEAD_DIM] for h in win_heads], axis=1)
            mix_w = (conv_w_out_b, win_w_out_b, na_w_out_b, w_out_b)
            cross_w = (cross_w_q_b, cross_w_o_b)
        (z, bq, bk, bv, cq, ck, cv), cast = _inproj(xt, row(norm_mix[l]), w_in_b, w_bq, l,
                                                    side=[flat(w) for w in late_ffn] if l == 0 else ())
        if l == 0:
            ffn2 = [c.reshape(w.shape) for c, w in zip(cast, late_ffn)]

        def seqs(a):
            return a.reshape(bsz, seq, a.shape[-1])

        conv_p = (conv_dw_w[l].astype(F32), row(conv_dw_b[l]), row(conv_ln_g[l]), row(conv_ln_b[l]))
        ob, oc = _attn(seqs(bq), seqs(bk), jnp.swapaxes(seqs(bv), 1, 2), win_sink[l].astype(F32) * LOG2E, win_bias,
                       seqs(cq), seqs(ck), seqs(cv), _na_bias(na_rpb[l] * LOG2E))
        xt = _mixout(xt, row(norm_mix[l]), z, conv_p, ob.reshape(t, -1), oc.reshape(t, -1), w_in_b,
                     *mix_w, l, seq)
        mk, mv = _memkv(mem, row(norm_mem[l]), cross_w_kv_b, l)
        xt = _cross(xt.reshape(bsz, seq, d), row(norm_cross[l]), mk, mv, *cross_w, l).reshape(t, d)
        xt, _ = _ffn(xt, row(norm_ffn2[l]), *ffn2, gf, l == depth - 1, l)
    return xt.reshape(bsz, seq, d)
```

```python
import functools

import numpy as np
import jax
import jax.numpy as jnp
from jax import lax
from jax.experimental import pallas as pl
from jax.experimental.pallas import tpu as pltpu

F32 = jnp.float32
BF16 = jnp.bfloat16

D_MODEL = 1024
EPS = 1e-6
LOG2E = 1.4426950408889634
NEG_INF = -1e30
HEAD_DIM = 64
D_FF = 11 * D_MODEL // 4
CONV_CH = D_MODEL // 4
CONV_WIDTH = 31
WIN_HEADS = (D_MODEL // 2) // HEAD_DIM
WIN_KV_HEADS = 2
WIN_GROUP = WIN_HEADS // WIN_KV_HEADS
WINDOW = 128
BLOCK = 128
T5_BUCKETS = 32
T5_MAX_DIST = 128
NA_HEADS = (D_MODEL // 4) // HEAD_DIM
NA_ROWS = 8
NA_COLS = 16
GRID_W = 64
X_HEADS = 4
X_HEAD_DIM = D_MODEL // X_HEADS
N_BRANCH = 3

WIN_Q = WIN_HEADS * HEAD_DIM
WIN_KV = WIN_KV_HEADS * HEAD_DIM
NA_W = NA_HEADS * HEAD_DIM
OFF_CONV = 0
OFF_BQ = OFF_CONV + 2 * CONV_CH
OFF_BK = OFF_BQ + WIN_Q
OFF_BV = OFF_BK + WIN_KV
OFF_CQ = OFF_BV + WIN_KV
OFF_CK = OFF_CQ + NA_W
OFF_CV = OFF_CK + NA_W
OFF_GATE = OFF_CV + NA_W
IN_WIDTH = OFF_GATE + N_BRANCH * D_MODEL

VMEM_LIMIT_BYTES = 56 * 1024 * 1024
TOKEN_TILE = 1024
FF_CHUNK = 256
MIX_CHUNK = 256
CONV_ROWS = 128
CONV_PAD = 16
NA_Q_ROWS = 32
NA_UNROLL = 32


def _params(*sem, flags=None):
    return pltpu.CompilerParams(dimension_semantics=sem, vmem_limit_bytes=VMEM_LIMIT_BYTES, flags=flags)


def _resident(shape):
    zeros = (0,) * len(shape)
    return pl.BlockSpec(shape, lambda *_: zeros, pipeline_mode=pl.Buffered(1))


def _layer(w, l, cols=None):
    shape = w.shape[1:] if cols is None else w.shape[1:-1] + (cols,)
    zeros = (0,) * len(shape)
    return pl.BlockSpec((None,) + shape, lambda *_: (l,) + zeros, pipeline_mode=pl.Buffered(1))


def _side_specs(side, steps):
    packed_rows = 16
    for a in side:
        assert a.ndim == 2 and a.shape[0] % (steps * packed_rows) == 0
    return [pl.BlockSpec((a.shape[0] // steps, a.shape[1]), lambda i: (i, 0)) for a in side]


def _side_shapes(side):
    return [jax.ShapeDtypeStruct(a.shape, BF16) for a in side]


def _cast_side(src_refs, dst_refs):
    for src, dst in zip(src_refs, dst_refs):
        dst[...] = src[...].astype(BF16)


def _rms(x, g):
    ms = jnp.mean(x * x, axis=-1, keepdims=True)
    return x * lax.rsqrt(ms + EPS) * g


def _dot(a, b):
    return jnp.dot(a, b, preferred_element_type=F32)


def _dot_nt(a, b):
    return lax.dot_general(a, b, (((1,), (1,)), ((), ())), preferred_element_type=F32)


def _ffn_kernel(*refs, final_norm, n_side):
    x_ref, g_ref, wg_ref, wu_ref, wd_ref, gf_ref = refs[:6]
    o_ref, act_ref = refs[6 + n_side], refs[-1]
    _cast_side(refs[6:6 + n_side], refs[7 + n_side:7 + 2 * n_side])
    x = x_ref[...]
    h = _rms(x, g_ref[...]).astype(BF16)
    for c in range(D_FF // FF_CHUNK):
        sl = slice(c * FF_CHUNK, (c + 1) * FF_CHUNK)
        a = _dot(h, wg_ref[:, sl])
        b = _dot(h, wu_ref[:, sl])
        act_ref[:, sl] = (a * jax.nn.sigmoid(a) * b).astype(BF16)
    y = x + 0.5 * _dot(act_ref[...], wd_ref[...])
    if final_norm:
        y = _rms(y, gf_ref[...])
    o_ref[...] = y


def _ffn(x, g, wg, wu, wd, gf, final_norm, l, side=()):
    t, d = x.shape
    steps = t // TOKEN_TILE
    tile = pl.BlockSpec((TOKEN_TILE, d), lambda i: (i, 0))
    out = pl.pallas_call(
        functools.partial(_ffn_kernel, final_norm=final_norm, n_side=len(side)),
        grid=(steps,),
        in_specs=[tile, _resident((1, d)), _layer(wg, l), _layer(wu, l), _layer(wd, l), _resident((1, d))]
                 + _side_specs(side, steps),
        out_specs=[tile] + _side_specs(side, steps),
        out_shape=[jax.ShapeDtypeStruct((t, d), F32)] + _side_shapes(side),
        scratch_shapes=[pltpu.VMEM((TOKEN_TILE, D_FF), BF16)],
        compiler_params=_params("parallel"),
        name="ffn",
    )(x, g, wg, wu, wd, gf, *side)
    return out[0], out[1:]


def _inproj_kernel(*refs, n_side):
    x_ref, g_ref, w_ref, wq_ref = refs[:4]
    z_ref, bq_ref, bk_ref, bv_ref, cq_ref, ck_ref, cv_ref = refs[4 + n_side:11 + n_side]
    _cast_side(refs[4:4 + n_side], refs[11 + n_side:])
    h = _rms(x_ref[...], g_ref[...]).astype(BF16)
    scale = HEAD_DIM ** -0.5 * LOG2E
    ua = _dot(h, w_ref[:, OFF_CONV:OFF_BQ])
    z_ref[...] = ua[:, :CONV_CH] * jax.nn.sigmoid(ua[:, CONV_CH:])
    bq_ref[...] = (_dot(h, wq_ref[...]) * scale).astype(BF16)
    bkv = _dot(h, w_ref[:, OFF_BK:OFF_CQ])
    bk_ref[...] = bkv[:, :WIN_KV].astype(BF16)
    bv_ref[...] = bkv[:, WIN_KV:].astype(BF16)
    cq_ref[...] = (_dot(h, w_ref[:, OFF_CQ:OFF_CK]) * scale).astype(BF16)
    ck_ref[...] = _dot(h, w_ref[:, OFF_CK:OFF_CV]).astype(BF16)
    cv_ref[...] = _dot(h, w_ref[:, OFF_CV:OFF_GATE]).astype(BF16)


def _inproj(x, g, w, wq, l, side=()):
    t, d = x.shape
    steps = t // TOKEN_TILE
    widths = (CONV_CH, WIN_Q, WIN_KV, WIN_KV, NA_W, NA_W, NA_W)
    dtypes = (F32,) + (BF16,) * 6

    def tile(n):
        return pl.BlockSpec((TOKEN_TILE, n), lambda i: (i, 0))

    out = pl.pallas_call(
        functools.partial(_inproj_kernel, n_side=len(side)),
        grid=(steps,),
        in_specs=[tile(d), _resident((1, d)), _layer(w, l, cols=OFF_GATE), _layer(wq, l)]
                 + _side_specs(side, steps),
        out_specs=[tile(n) for n in widths] + _side_specs(side, steps),
        out_shape=[jax.ShapeDtypeStruct((t, n), dt) for n, dt in zip(widths, dtypes)] + _side_shapes(side),
        compiler_params=_params("parallel"),
        name="inproj",
    )(x, g, w, wq, *side)
    return out[:len(widths)], out[len(widths):]


CONV_SLAB = CONV_ROWS + 2 * CONV_PAD


def _conv_chunk(slab, w, b, lg, lb):
    half = CONV_WIDTH // 2
    lead = CONV_PAD - half
    n_shift = 8
    span = CONV_SLAB - n_shift
    acc = jnp.zeros((CONV_ROWS, CONV_CH), F32) + b
    for s in range(n_shift):
        shifted = slab if s == 0 else pltpu.roll(slab, CONV_SLAB - s, axis=0)
        for a in range(0, span - CONV_ROWS + 1, n_shift):
            k = a + s - lead
            if 0 <= k < CONV_WIDTH:
                acc = acc + shifted[a:a + CONV_ROWS, :] * w[k:k + 1, :]
    mu = jnp.mean(acc, axis=-1, keepdims=True)
    cen = acc - mu
    var = jnp.mean(cen * cen, axis=-1, keepdims=True)
    y = cen * lax.rsqrt(var + EPS) * lg + lb
    return (y * jax.nn.sigmoid(y)).astype(BF16)


def _t5_buckets(rel):
    half = T5_BUCKETS // 2
    max_exact = half // 2
    ret = (rel > 0).astype(np.int32) * half
    n = np.abs(rel)
    large = max_exact + (np.log(np.maximum(n, 1) / max_exact)
                         / np.log(T5_MAX_DIST / max_exact) * (half - max_exact)).astype(np.int32)
    large = np.minimum(large, half - 1)
    return ret + np.where(n < max_exact, n, large)


WIN_PAIRS = WIN_HEADS // WIN_KV_HEADS
WIN_SPAN = BLOCK + 2 * WINDOW
WIN_QB = 16
WIN_UNROLL = 16
assert WIN_KV_HEADS == 2 and 2 * HEAD_DIM == 128


def _win_bias(t5_table):
    n = WIN_SPAN + BLOCK
    rel = np.arange(n) - (n // 2 - 1)
    line = jnp.where((np.abs(rel) <= WINDOW)[:, None], t5_table[_t5_buckets(rel)].astype(F32), NEG_INF)
    line = jnp.transpose(line)
    skew = jnp.tile(line, (1, BLOCK))[:, :BLOCK * (n - 1)].reshape(WIN_HEADS, BLOCK, n - 1)
    bias = skew[:, :, BLOCK - 1:BLOCK - 1 + WIN_SPAN]
    bias_t = jnp.transpose(bias, (0, 2, 1))
    key = np.arange(WIN_SPAN)[None, :, None]
    first = jnp.where(key < WINDOW, NEG_INF, bias_t)
    last = jnp.where(key >= WINDOW + BLOCK, NEG_INF, bias_t)
    return jnp.stack([first, bias_t, last])


def _win_kernel(sink_ref, q_ref, kp_ref, kc_ref, kn_ref, vp_ref, vc_ref, vn_ref, bias_ref, o_ref,
                k_scr, vt_scr, *, nblk):
    i = pl.program_id(1)
    k_scr[0] = kp_ref[0]
    vt_scr[0] = vp_ref[0]
    for b in range(WIN_QB):
        k_scr[1 + b] = kc_ref[0, b * BLOCK:(b + 1) * BLOCK, :]
        vt_scr[1 + b] = vc_ref[0, :, b * BLOCK:(b + 1) * BLOCK]
    k_scr[WIN_QB + 1] = kn_ref[0]
    vt_scr[WIN_QB + 1] = vn_ref[0]

    def block(jb, carry):
        g = i * WIN_QB + jb
        variant = jnp.where(g == 0, 0, jnp.where(g == nblk - 1, 2, 1))
        rows = pl.ds(pl.multiple_of(jb * BLOCK, BLOCK), BLOCK)
        k = jnp.concatenate([k_scr[jb], k_scr[jb + 1], k_scr[jb + 2]], axis=0)
        vt = jnp.concatenate([vt_scr[jb], vt_scr[jb + 1], vt_scr[jb + 2]], axis=1)
        low = lax.broadcasted_iota(jnp.int32, k.shape, 1) < HEAD_DIM
        k_half = (jnp.where(low, k, 0), jnp.where(low, 0, k))
        ch = lax.broadcasted_iota(jnp.int32, vt.shape, 0)
        den_row = (HEAD_DIM, 0)
        vt_half = (jnp.where(ch < HEAD_DIM, vt, jnp.where(ch == den_row[0], 1, 0).astype(BF16)),
                   jnp.where(ch >= HEAD_DIM, vt, jnp.where(ch == den_row[1], 1, 0).astype(BF16)))
        for t in range(WIN_PAIRS):
            qp = q_ref[0, rows, t * 128:(t + 1) * 128]
            halves = []
            for j in range(WIN_KV_HEADS):
                h = t + j * WIN_GROUP
                sink = sink_ref[h]
                s = _dot_nt(k_half[j], qp) + bias_ref[variant, h]
                m = jnp.maximum(jnp.max(s, axis=0, keepdims=True), sink)
                p = jnp.exp2(s - m).astype(BF16)
                o = _dot(vt_half[j], p)
                den = o[den_row[j]:den_row[j] + 1] + jnp.exp2(sink - m)
                halves.append(o[j * HEAD_DIM:(j + 1) * HEAD_DIM] * (1.0 / den))
            o_ref[0, rows, t * 128:(t + 1) * 128] = jnp.concatenate(halves, axis=0).T.astype(BF16)
        return carry

    lax.fori_loop(0, WIN_QB, block, 0, unroll=WIN_UNROLL)


def _na_bias(rpb):
    c = np.arange(GRID_W)
    col_start = np.clip(c - NA_COLS // 2, 0, GRID_W - NA_COLS)
    col = np.arange(GRID_W)
    inside = (col[None, :] >= col_start[:, None]) & (col[None, :] < col_start[:, None] + NA_COLS)
    b = rpb.astype(F32)
    n = 2 * GRID_W
    lead = GRID_W - NA_COLS
    line = jnp.pad(b, ((0, 0),) * 2 + ((lead, n - lead - (2 * NA_COLS - 1)),))
    skew = jnp.tile(line, (1, 1, GRID_W))[..., :GRID_W * (n - 1)]
    skew = skew.reshape(b.shape[:2] + (GRID_W, n - 1))[..., GRID_W - 1:2 * GRID_W - 1]
    b = jnp.where(inside[None, None], skew, NEG_INF)
    b = b.reshape((NA_HEADS // 2, 2) + b.shape[1:])
    b = jnp.transpose(b, (0, 2, 4, 1, 3))
    return b.reshape(NA_HEADS // 2, (2 * NA_ROWS - 1) * GRID_W, 2 * GRID_W)


def _na_kernel(q_ref, k_ref, v_ref, bias_ref, o_ref, *, rows):
    j = pl.program_id(1)
    win = NA_ROWS * GRID_W
    low = lax.broadcasted_iota(jnp.int32, (GRID_W, 2 * HEAD_DIM), 1) < HEAD_DIM

    def row(t, carry):
        r = j * NA_Q_ROWS + t
        rs = jnp.clip(r - NA_ROWS // 2, 0, rows - NA_ROWS)
        keys = pl.ds(pl.multiple_of(rs * GRID_W, GRID_W), win)
        qrows = pl.ds(pl.multiple_of(t * GRID_W, GRID_W), GRID_W)
        brows = pl.ds(pl.multiple_of((NA_ROWS - 1 - (r - rs)) * GRID_W, GRID_W), win)
        for tile in range(NA_HEADS // 2):
            lanes = slice(tile * 2 * HEAD_DIM, (tile + 1) * 2 * HEAD_DIM)
            q = q_ref[0, qrows, lanes]
            q_bd = jnp.concatenate([jnp.where(low, q, 0), jnp.where(low, 0, q)], axis=0)
            s = _dot_nt(k_ref[0, keys, lanes], q_bd) + bias_ref[tile, brows, :]
            m = jnp.max(s, axis=0, keepdims=True)
            p = jnp.exp2(s - m)
            den = jnp.sum(p, axis=0, keepdims=True)
            o = lax.dot_general(v_ref[0, keys, lanes], p.astype(BF16), (((0,), (0,)), ((), ())),
                                preferred_element_type=F32)
            ot = (o * (1.0 / den)).T
            o_ref[0, qrows, lanes] = jnp.where(low, ot[:GRID_W], ot[GRID_W:]).astype(BF16)
        return carry

    lax.fori_loop(0, NA_Q_ROWS, row, 0, unroll=NA_UNROLL)


def _attn_kernel(*refs, nblk, rows):
    _win_kernel(*refs[:9], refs[13], refs[15], refs[16], nblk=nblk)
    _na_kernel(*refs[9:13], refs[14], rows=rows)


def _attn(bq, bk, bvt, sink, win_bias, cq, ck, cv, na_bias):
    bsz, seq, _ = bq.shape
    nblk = seq // BLOCK
    rows = seq // GRID_W
    assert nblk >= 2 and nblk % WIN_QB == 0 and WIN_QB * BLOCK == NA_Q_ROWS * GRID_W

    def edge(shift):
        return lambda b, i: jnp.clip(i * WIN_QB + shift, 0, nblk - 1)

    def k_edge(shift):
        return pl.BlockSpec((1, BLOCK, WIN_KV), lambda b, i: (b, edge(shift)(b, i), 0))

    def v_edge(shift):
        return pl.BlockSpec((1, WIN_KV, BLOCK), lambda b, i: (b, 0, edge(shift)(b, i)))

    span = WIN_QB * BLOCK
    qspec = pl.BlockSpec((1, span, WIN_Q), lambda b, i: (b, i, 0))
    cqspec = pl.BlockSpec((1, span, NA_W), lambda b, i: (b, i, 0))
    ckvspec = pl.BlockSpec((1, seq, NA_W), lambda b, i: (b, 0, 0))
    return pl.pallas_call(
        functools.partial(_attn_kernel, nblk=nblk, rows=rows),
        grid=(bsz, nblk // WIN_QB),
        in_specs=[pl.BlockSpec(memory_space=pltpu.SMEM), qspec,
                  k_edge(-1), pl.BlockSpec((1, span, WIN_KV), lambda b, i: (b, i, 0)), k_edge(WIN_QB),
                  v_edge(-1), pl.BlockSpec((1, WIN_KV, span), lambda b, i: (b, 0, i)), v_edge(WIN_QB),
                  _resident(win_bias.shape),
                  cqspec, ckvspec, ckvspec, _resident(na_bias.shape)],
        out_specs=[qspec, cqspec],
        out_shape=[jax.ShapeDtypeStruct((bsz, seq, WIN_Q), BF16), jax.ShapeDtypeStruct((bsz, seq, NA_W), BF16)],
        scratch_shapes=[pltpu.VMEM((WIN_QB + 2, BLOCK, WIN_KV), BF16),
                        pltpu.VMEM((WIN_QB + 2, WIN_KV, BLOCK), BF16)],
        compiler_params=_params("parallel", "arbitrary"),
        name="attn",
    )(sink, bq, bk, bk, bk, bvt, bvt, bvt, win_bias, cq, ck, cv, na_bias)


def _mixout_kernel(x_ref, g_ref, zprev_ref, z_ref, znext_ref, cw_ref, cb_ref, clg_ref, clb_ref,
                   ob_ref, oc_ref, wga_ref, wgb_ref, wgc_ref, wa_ref, wb_ref, wc_ref, wo_ref,
                   o_ref, y_ref, *, tiles_per_seq):
    x = x_ref[...]
    h = _rms(x, g_ref[...]).astype(BF16)
    chunks = [slice(c * MIX_CHUNK, (c + 1) * MIX_CHUNK) for c in range(D_MODEL // MIX_CHUNK)]

    def gated(wg_ref, b_ref, w_ref, sl):
        return jax.nn.sigmoid(_dot(h, wg_ref[:, sl])) * _dot(b_ref[...], w_ref[:, sl])

    pos = pl.program_id(0) % tiles_per_seq
    before = jnp.where(pos != 0, zprev_ref[...], 0.0)
    after = jnp.where(pos != tiles_per_seq - 1, znext_ref[...], 0.0)
    n_chunks = TOKEN_TILE // CONV_ROWS
    conv_args = (cw_ref[...], cb_ref[...], clg_ref[...], clb_ref[...])

    def conv(c):
        lo, hi = c * CONV_ROWS - CONV_PAD, (c + 1) * CONV_ROWS + CONV_PAD
        parts = ([before] if c == 0 else []) + [z_ref[max(lo, 0):min(hi, TOKEN_TILE), :]] \
            + ([after] if c == n_chunks - 1 else [])
        slab = parts[0] if len(parts) == 1 else jnp.concatenate(parts, axis=0)
        return _conv_chunk(slab, *conv_args)

    za = jnp.concatenate([conv(c) for c in range(n_chunks)], axis=0)
    for sl in chunks:
        y = gated(wgb_ref, ob_ref, wb_ref, sl) + gated(wgc_ref, oc_ref, wc_ref, sl)
        y = y + jax.nn.sigmoid(_dot(h, wga_ref[:, sl])) * _dot(za, wa_ref[:, sl])
        y_ref[:, sl] = y.astype(BF16)
    o_ref[...] = x + _dot(y_ref[...], wo_ref[...])


def _mixout(x, g, z, conv_p, ob, oc, w_in, wa, wb, wc, wo, l, seq):
    t, d = x.shape
    assert OFF_GATE % d == 0 and seq % TOKEN_TILE == 0 and TOKEN_TILE % CONV_ROWS == 0
    halo_per_tile = TOKEN_TILE // CONV_PAD
    n_halo = t // CONV_PAD

    def tile(n):
        return pl.BlockSpec((TOKEN_TILE, n), lambda i: (i, 0))

    def gate_cols(c):
        blk = OFF_GATE // d + c
        return pl.BlockSpec((None, d, d), lambda i: (l, 0, blk), pipeline_mode=pl.Buffered(1))

    z_before = pl.BlockSpec((CONV_PAD, CONV_CH), lambda i: (jnp.maximum(i * halo_per_tile - 1, 0), 0))
    z_after = pl.BlockSpec((CONV_PAD, CONV_CH), lambda i: (jnp.minimum((i + 1) * halo_per_tile, n_halo - 1), 0))
    return pl.pallas_call(
        functools.partial(_mixout_kernel, tiles_per_seq=seq // TOKEN_TILE),
        grid=(t // TOKEN_TILE,),
        in_specs=[tile(d), _resident((1, d)), z_before, tile(CONV_CH), z_after]
                 + [_resident(p.shape) for p in conv_p]
                 + [tile(WIN_Q), tile(NA_W), gate_cols(0), gate_cols(1), gate_cols(2),
                    _layer(wa, l), _layer(wb, l), _layer(wc, l), _layer(wo, l)],
        out_specs=tile(d),
        out_shape=jax.ShapeDtypeStruct((t, d), F32),
        scratch_shapes=[pltpu.VMEM((TOKEN_TILE, d), BF16)],
        compiler_params=_params("parallel"),
        name="mixout",
    )(x, g, z, z, z, *conv_p, ob, oc, w_in, w_in, w_in, wa, wb, wc, wo)


def _cross_kernel(x_ref, g_ref, m_ref, gm_ref, wkv_ref, wq_ref, wo_ref, o_ref, k_ref, v_ref):
    @pl.when(pl.program_id(1) == 0)
    def _():
        hm = _rms(m_ref[0], gm_ref[...]).astype(BF16)
        k_ref[...] = _dot(hm, wkv_ref[:, :D_MODEL]).astype(BF16)
        v_ref[...] = _dot(hm, wkv_ref[:, D_MODEL:]).astype(BF16)

    x = x_ref[0]
    h = _rms(x, g_ref[...]).astype(BF16)
    scale = X_HEAD_DIM ** -0.5
    q = (_dot(h, wq_ref[...]) * scale).astype(BF16)
    k = k_ref[...]
    v = v_ref[...]
    outs = []
    for hd in range(X_HEADS):
        hs = slice(hd * X_HEAD_DIM, (hd + 1) * X_HEAD_DIM)
        s = _dot_nt(q[:, hs], k[:, hs])
        m = jnp.max(s, axis=-1, keepdims=True)
        p = jnp.exp(s - m)
        den = jnp.sum(p, axis=-1, keepdims=True)
        outs.append((_dot(p.astype(BF16), v[:, hs]) / den).astype(BF16))
    o = jnp.concatenate(outs, axis=-1)
    o_ref[0] = x + _dot(o, wo_ref[...])


def _cross(x, g, mem, gm, wkv, wq, wo, l):
    bsz, seq, d = x.shape
    mlen = mem.shape[1]
    tile = pl.BlockSpec((1, TOKEN_TILE, d), lambda b, i: (b, i, 0))
    mblk = pl.BlockSpec((1, mlen, d), lambda b, i: (b, 0, 0))
    return pl.pallas_call(
        _cross_kernel,
        grid=(bsz, seq // TOKEN_TILE),
        in_specs=[tile, _resident((1, d)), mblk, _resident((1, d)), _layer(wkv, l), _layer(wq, l), _layer(wo, l)],
        out_specs=tile,
        out_shape=jax.ShapeDtypeStruct((bsz, seq, d), F32),
        scratch_shapes=[pltpu.VMEM((mlen, d), BF16), pltpu.VMEM((mlen, d), BF16)],
        compiler_params=_params("parallel", "arbitrary"),
        name="cross",
    )(x, g, mem, gm, wkv, wq, wo)


def kernel(x, mem, norm_ffn1, ffn1_w_gate, ffn1_w_up, ffn1_w_down, norm_mix, w_in, conv_dw_w, conv_dw_b, conv_ln_g, conv_ln_b, conv_w_out, win_sink, t5_bias, win_w_out, na_rpb, na_w_out, w_out, norm_cross, norm_mem, cross_w_q, cross_w_kv, cross_w_o, norm_ffn2, ffn2_w_gate, ffn2_w_up, ffn2_w_down, norm_final):
    bsz, seq, d = x.shape
    depth = w_in.shape[0]
    t = bsz * seq
    assert d == D_MODEL and t % TOKEN_TILE == 0 and seq % TOKEN_TILE == 0
    assert seq % (NA_Q_ROWS * GRID_W) == 0 and seq // GRID_W >= NA_ROWS and seq % CONV_ROWS == 0

    def row(v):
        return v.reshape(1, -1).astype(F32)

    def bf(w):
        return w.astype(BF16)

    win_bias = _win_bias(t5_bias * LOG2E)
    win_heads = [h for tl in range(WIN_PAIRS) for h in (tl, tl + WIN_GROUP)]
    gf = row(norm_final)
    ffn1 = (bf(ffn1_w_gate), bf(ffn1_w_up), bf(ffn1_w_down))
    late = (w_in, conv_w_out, win_w_out, na_w_out, w_out, cross_w_q, cross_w_kv, cross_w_o)
    late_ffn = (ffn2_w_gate, ffn2_w_up, ffn2_w_down)

    def flat(w):
        return w.reshape(-1, w.shape[-1]).astype(F32)

    xt = x.reshape(t, d)
    for l in range(depth):
        xt, cast = _ffn(xt, row(norm_ffn1[l]), *ffn1, gf, False, l,
                        side=[flat(w) for w in late] if l == 0 else ())
        if l == 0:
            (w_in_b, conv_w_out_b, win_w_out_b, na_w_out_b, w_out_b, cross_w_q_b, cross_w_kv_b,
             cross_w_o_b) = [c.reshape(w.shape) for c, w in zip(cast, late)]
            w_bq = jnp.concatenate(
                [w_in_b[:, :, OFF_BQ + h * HEAD_DIM:OFF_BQ + (h + 1) * HEAD_DIM] for h in win_heads], axis=2)
            win_w_out_b = jnp.concatenate(
                [win_w_out_b[:, h * HEAD_DIM:(h + 1) * HEAD_DIM] for h in win_heads], axis=1)
            mix_w = (conv_w_out_b, win_w_out_b, na_w_out_b, w_out_b)
            cross_w = (cross_w_q_b, cross_w_o_b)
        (z, bq, bk, bv, cq, ck, cv), cast = _inproj(xt, row(norm_mix[l]), w_in_b, w_bq, l,
                                                    side=[flat(w) for w in late_ffn] if l == 0 else ())
        if l == 0:
            ffn2 = [c.reshape(w.shape) for c, w in zip(cast, late_ffn)]

        def seqs(a):
            return a.reshape(bsz, seq, a.shape[-1])

        conv_p = (conv_dw_w[l].astype(F32), row(conv_dw_b[l]), row(conv_ln_g[l]), row(conv_ln_b[l]))
        ob, oc = _attn(seqs(bq), seqs(bk), jnp.swapaxes(seqs(bv), 1, 2), win_sink[l].astype(F32) * LOG2E, win_bias,
                       seqs(cq), seqs(ck), seqs(cv), _na_bias(na_rpb[l] * LOG2E))
        xt = _mixout(xt, row(norm_mix[l]), z, conv_p, ob.reshape(t, -1), oc.reshape(t, -1), w_in_b,
                     *mix_w, l, seq)
        xt = _cross(xt.reshape(bsz, seq, d), row(norm_cross[l]), mem.astype(F32), row(norm_mem[l]),
                    cross_w_kv_b, *cross_w, l).reshape(t, d)
        xt, _ = _ffn(xt, row(norm_ffn2[l]), *ffn2, gf, l == depth - 1, l)
    return xt.reshape(bsz, seq, d)
```

```python
import functools

import numpy as np
import jax
import jax.numpy as jnp
from jax import lax
from jax.experimental import pallas as pl
from jax.experimental.pallas import tpu as pltpu

F32 = jnp.float32
BF16 = jnp.bfloat16

D_MODEL = 1024
EPS = 1e-6
LOG2E = 1.4426950408889634
NEG_INF = -1e30
HEAD_DIM = 64
D_FF = 11 * D_MODEL // 4
CONV_CH = D_MODEL // 4
CONV_WIDTH = 31
WIN_HEADS = (D_MODEL // 2) // HEAD_DIM
WIN_KV_HEADS = 2
WIN_GROUP = WIN_HEADS // WIN_KV_HEADS
WINDOW = 128
BLOCK = 128
T5_BUCKETS = 32
T5_MAX_DIST = 128
NA_HEADS = (D_MODEL // 4) // HEAD_DIM
NA_ROWS = 8
NA_COLS = 16
GRID_W = 64
X_HEADS = 4
X_HEAD_DIM = D_MODEL // X_HEADS
N_BRANCH = 3

WIN_Q = WIN_HEADS * HEAD_DIM
WIN_KV = WIN_KV_HEADS * HEAD_DIM
NA_W = NA_HEADS * HEAD_DIM
OFF_CONV = 0
OFF_BQ = OFF_CONV + 2 * CONV_CH
OFF_BK = OFF_BQ + WIN_Q
OFF_BV = OFF_BK + WIN_KV
OFF_CQ = OFF_BV + WIN_KV
OFF_CK = OFF_CQ + NA_W
OFF_CV = OFF_CK + NA_W
OFF_GATE = OFF_CV + NA_W
IN_WIDTH = OFF_GATE + N_BRANCH * D_MODEL

VMEM_LIMIT_BYTES = 56 * 1024 * 1024
TOKEN_TILE = 1024
FF_CHUNK = 256
MIX_CHUNK = 256
CONV_ROWS = 128
CONV_PAD = 16
NA_Q_ROWS = 32
NA_UNROLL = 32


def _params(*sem, flags=None):
    return pltpu.CompilerParams(dimension_semantics=sem, vmem_limit_bytes=VMEM_LIMIT_BYTES, flags=flags)


def _resident(shape):
    zeros = (0,) * len(shape)
    return pl.BlockSpec(shape, lambda *_: zeros, pipeline_mode=pl.Buffered(1))


def _layer(w, l, cols=None):
    shape = w.shape[1:] if cols is None else w.shape[1:-1] + (cols,)
    zeros = (0,) * len(shape)
    return pl.BlockSpec((None,) + shape, lambda *_: (l,) + zeros, pipeline_mode=pl.Buffered(1))


def _side_specs(side, steps):
    packed_rows = 16
    for a in side:
        assert a.ndim == 2 and a.shape[0] % (steps * packed_rows) == 0
    return [pl.BlockSpec((a.shape[0] // steps, a.shape[1]), lambda i: (i, 0)) for a in side]


def _side_shapes(side):
    return [jax.ShapeDtypeStruct(a.shape, BF16) for a in side]


def _cast_side(src_refs, dst_refs):
    for src, dst in zip(src_refs, dst_refs):
        dst[...] = src[...].astype(BF16)


def _rms(x, g):
    ms = jnp.mean(x * x, axis=-1, keepdims=True)
    return x * lax.rsqrt(ms + EPS) * g


def _dot(a, b):
    return jnp.dot(a, b, preferred_element_type=F32)


def _dot_nt(a, b):
    return lax.dot_general(a, b, (((1,), (1,)), ((), ())), preferred_element_type=F32)


def _ffn_kernel(*refs, final_norm, n_side):
    x_ref, g_ref, wg_ref, wu_ref, wd_ref, gf_ref = refs[:6]
    o_ref, act_ref = refs[6 + n_side], refs[-1]
    _cast_side(refs[6:6 + n_side], refs[7 + n_side:7 + 2 * n_side])
    x = x_ref[...]
    h = _rms(x, g_ref[...]).astype(BF16)
    for c in range(D_FF // FF_CHUNK):
        sl = slice(c * FF_CHUNK, (c + 1) * FF_CHUNK)
        a = _dot(h, wg_ref[:, sl])
        b = _dot(h, wu_ref[:, sl])
        act_ref[:, sl] = (a * jax.nn.sigmoid(a) * b).astype(BF16)
    y = x + 0.5 * _dot(act_ref[...], wd_ref[...])
    if final_norm:
        y = _rms(y, gf_ref[...])
    o_ref[...] = y


def _ffn(x, g, wg, wu, wd, gf, final_norm, l, side=()):
    t, d = x.shape
    steps = t // TOKEN_TILE
    tile = pl.BlockSpec((TOKEN_TILE, d), lambda i: (i, 0))
    out = pl.pallas_call(
        functools.partial(_ffn_kernel, final_norm=final_norm, n_side=len(side)),
        grid=(steps,),
        in_specs=[tile, _resident((1, d)), _layer(wg, l), _layer(wu, l), _layer(wd, l), _resident((1, d))]
                 + _side_specs(side, steps),
        out_specs=[tile] + _side_specs(side, steps),
        out_shape=[jax.ShapeDtypeStruct((t, d), F32)] + _side_shapes(side),
        scratch_shapes=[pltpu.VMEM((TOKEN_TILE, D_FF), BF16)],
        compiler_params=_params("parallel"),
        name="ffn",
    )(x, g, wg, wu, wd, gf, *side)
    return out[0], out[1:]


def _inproj_kernel(*refs, n_side):
    x_ref, g_ref, w_ref, wq_ref = refs[:4]
    z_ref, bq_ref, bk_ref, bv_ref, cq_ref, ck_ref, cv_ref = refs[4 + n_side:11 + n_side]
    _cast_side(refs[4:4 + n_side], refs[11 + n_side:])
    h = _rms(x_ref[...], g_ref[...]).astype(BF16)
    scale = HEAD_DIM ** -0.5 * LOG2E
    ua = _dot(h, w_ref[:, OFF_CONV:OFF_BQ])
    z_ref[...] = ua[:, :CONV_CH] * jax.nn.sigmoid(ua[:, CONV_CH:])
    bq_ref[...] = (_dot(h, wq_ref[...]) * scale).astype(BF16)
    bkv = _dot(h, w_ref[:, OFF_BK:OFF_CQ])
    bk_ref[...] = bkv[:, :WIN_KV].astype(BF16)
    bv_ref[...] = bkv[:, WIN_KV:].astype(BF16)
    cq_ref[...] = (_dot(h, w_ref[:, OFF_CQ:OFF_CK]) * scale).astype(BF16)
    ck_ref[...] = _dot(h, w_ref[:, OFF_CK:OFF_CV]).astype(BF16)
    cv_ref[...] = _dot(h, w_ref[:, OFF_CV:OFF_GATE]).astype(BF16)


def _inproj(x, g, w, wq, l, side=()):
    t, d = x.shape
    steps = t // TOKEN_TILE
    widths = (CONV_CH, WIN_Q, WIN_KV, WIN_KV, NA_W, NA_W, NA_W)
    dtypes = (F32,) + (BF16,) * 6

    def tile(n):
        return pl.BlockSpec((TOKEN_TILE, n), lambda i: (i, 0))

    out = pl.pallas_call(
        functools.partial(_inproj_kernel, n_side=len(side)),
        grid=(steps,),
        in_specs=[tile(d), _resident((1, d)), _layer(w, l, cols=OFF_GATE), _layer(wq, l)]
                 + _side_specs(side, steps),
        out_specs=[tile(n) for n in widths] + _side_specs(side, steps),
        out_shape=[jax.ShapeDtypeStruct((t, n), dt) for n, dt in zip(widths, dtypes)] + _side_shapes(side),
        compiler_params=_params("parallel"),
        name="inproj",
    )(x, g, w, wq, *side)
    return out[:len(widths)], out[len(widths):]


CONV_SLAB = CONV_ROWS + 2 * CONV_PAD


def _conv_chunk(slab, w, b, lg, lb):
    half = CONV_WIDTH // 2
    lead = CONV_PAD - half
    n_shift = 8
    span = CONV_SLAB - n_shift
    acc = jnp.zeros((CONV_ROWS, CONV_CH), F32) + b
    for s in range(n_shift):
        shifted = slab if s == 0 else pltpu.roll(slab, CONV_SLAB - s, axis=0)
        for a in range(0, span - CONV_ROWS + 1, n_shift):
            k = a + s - lead
            if 0 <= k < CONV_WIDTH:
                acc = acc + shifted[a:a + CONV_ROWS, :] * w[k:k + 1, :]
    mu = jnp.mean(acc, axis=-1, keepdims=True)
    cen = acc - mu
    var = jnp.mean(cen * cen, axis=-1, keepdims=True)
    y = cen * lax.rsqrt(var + EPS) * lg + lb
    return (y * jax.nn.sigmoid(y)).astype(BF16)


def _t5_buckets(rel):
    half = T5_BUCKETS // 2
    max_exact = half // 2
    ret = (rel > 0).astype(np.int32) * half
    n = np.abs(rel)
    large = max_exact + (np.log(np.maximum(n, 1) / max_exact)
                         / np.log(T5_MAX_DIST / max_exact) * (half - max_exact)).astype(np.int32)
    large = np.minimum(large, half - 1)
    return ret + np.where(n < max_exact, n, large)


WIN_PAIRS = WIN_HEADS // WIN_KV_HEADS
WIN_SPAN = BLOCK + 2 * WINDOW
WIN_QB = 16
WIN_UNROLL = 16
assert WIN_KV_HEADS == 2 and 2 * HEAD_DIM == 128


def _win_bias(t5_table):
    n = WIN_SPAN + BLOCK
    rel = np.arange(n) - (n // 2 - 1)
    line = jnp.where((np.abs(rel) <= WINDOW)[:, None], t5_table[_t5_buckets(rel)].astype(F32), NEG_INF)
    line = jnp.transpose(line)
    skew = jnp.tile(line, (1, BLOCK))[:, :BLOCK * (n - 1)].reshape(WIN_HEADS, BLOCK, n - 1)
    bias = skew[:, :, BLOCK - 1:BLOCK - 1 + WIN_SPAN]
    bias_t = jnp.transpose(bias, (0, 2, 1))
    key = np.arange(WIN_SPAN)[None, :, None]
    first = jnp.where(key < WINDOW, NEG_INF, bias_t)
    last = jnp.where(key >= WINDOW + BLOCK, NEG_INF, bias_t)
    return jnp.stack([first, bias_t, last])


def _win_kernel(sink_ref, q_ref, kp_ref, kc_ref, kn_ref, vp_ref, vc_ref, vn_ref, bias_ref, o_ref,
                k_scr, vt_scr, *, nblk):
    i = pl.program_id(1)
    k_scr[0] = kp_ref[0]
    vt_scr[0] = vp_ref[0]
    for b in range(WIN_QB):
        k_scr[1 + b] = kc_ref[0, b * BLOCK:(b + 1) * BLOCK, :]
        vt_scr[1 + b] = vc_ref[0, :, b * BLOCK:(b + 1) * BLOCK]
    k_scr[WIN_QB + 1] = kn_ref[0]
    vt_scr[WIN_QB + 1] = vn_ref[0]

    def block(jb, carry):
        g = i * WIN_QB + jb
        variant = jnp.where(g == 0, 0, jnp.where(g == nblk - 1, 2, 1))
        rows = pl.ds(pl.multiple_of(jb * BLOCK, BLOCK), BLOCK)
        k = jnp.concatenate([k_scr[jb], k_scr[jb + 1], k_scr[jb + 2]], axis=0)
        vt = jnp.concatenate([vt_scr[jb], vt_scr[jb + 1], vt_scr[jb + 2]], axis=1)
        low = lax.broadcasted_iota(jnp.int32, k.shape, 1) < HEAD_DIM
        k_half = (jnp.where(low, k, 0), jnp.where(low, 0, k))
        ch = lax.broadcasted_iota(jnp.int32, vt.shape, 0)
        den_row = (HEAD_DIM, 0)
        vt_half = (jnp.where(ch < HEAD_DIM, vt, jnp.where(ch == den_row[0], 1, 0).astype(BF16)),
                   jnp.where(ch >= HEAD_DIM, vt, jnp.where(ch == den_row[1], 1, 0).astype(BF16)))
        for t in range(WIN_PAIRS):
            qp = q_ref[0, rows, t * 128:(t + 1) * 128]
            halves = []
            for j in range(WIN_KV_HEADS):
                h = t + j * WIN_GROUP
                sink = sink_ref[h]
                s = _dot_nt(k_half[j], qp) + bias_ref[variant, h]
                m = jnp.maximum(jnp.max(s, axis=0, keepdims=True), sink)
                p = jnp.exp2(s - m).astype(BF16)
                o = _dot(vt_half[j], p)
                den = o[den_row[j]:den_row[j] + 1] + jnp.exp2(sink - m)
                halves.append(o[j * HEAD_DIM:(j + 1) * HEAD_DIM] * (1.0 / den))
            o_ref[0, rows, t * 128:(t + 1) * 128] = jnp.concatenate(halves, axis=0).T.astype(BF16)
        return carry

    lax.fori_loop(0, WIN_QB, block, 0, unroll=WIN_UNROLL)


def _na_bias(rpb):
    c = np.arange(GRID_W)
    col_start = np.clip(c - NA_COLS // 2, 0, GRID_W - NA_COLS)
    col = np.arange(GRID_W)
    inside = (col[None, :] >= col_start[:, None]) & (col[None, :] < col_start[:, None] + NA_COLS)
    b = rpb.astype(F32)
    n = 2 * GRID_W
    lead = GRID_W - NA_COLS
    line = jnp.pad(b, ((0, 0),) * 2 + ((lead, n - lead - (2 * NA_COLS - 1)),))
    skew = jnp.tile(line, (1, 1, GRID_W))[..., :GRID_W * (n - 1)]
    skew = skew.reshape(b.shape[:2] + (GRID_W, n - 1))[..., GRID_W - 1:2 * GRID_W - 1]
    b = jnp.where(inside[None, None], skew, NEG_INF)
    b = b.reshape((NA_HEADS // 2, 2) + b.shape[1:])
    b = jnp.transpose(b, (0, 2, 4, 1, 3))
    return b.reshape(NA_HEADS // 2, (2 * NA_ROWS - 1) * GRID_W, 2 * GRID_W)


def _na_kernel(q_ref, k_ref, v_ref, bias_ref, o_ref, *, rows):
    j = pl.program_id(1)
    win = NA_ROWS * GRID_W
    low = lax.broadcasted_iota(jnp.int32, (GRID_W, 2 * HEAD_DIM), 1) < HEAD_DIM

    def row(t, carry):
        r = j * NA_Q_ROWS + t
        rs = jnp.clip(r - NA_ROWS // 2, 0, rows - NA_ROWS)
        keys = pl.ds(pl.multiple_of(rs * GRID_W, GRID_W), win)
        qrows = pl.ds(pl.multiple_of(t * GRID_W, GRID_W), GRID_W)
        brows = pl.ds(pl.multiple_of((NA_ROWS - 1 - (r - rs)) * GRID_W, GRID_W), win)
        for tile in range(NA_HEADS // 2):
            lanes = slice(tile * 2 * HEAD_DIM, (tile + 1) * 2 * HEAD_DIM)
            q = q_ref[0, qrows, lanes]
            q_bd = jnp.concatenate([jnp.where(low, q, 0), jnp.where(low, 0, q)], axis=0)
            s = _dot_nt(k_ref[0, keys, lanes], q_bd) + bias_ref[tile, brows, :]
            m = jnp.max(s, axis=0, keepdims=True)
            p = jnp.exp2(s - m)
            den = jnp.sum(p, axis=0, keepdims=True)
            o = lax.dot_general(v_ref[0, keys, lanes], p.astype(BF16), (((0,), (0,)), ((), ())),
                                preferred_element_type=F32)
            ot = (o * (1.0 / den)).T
            o_ref[0, qrows, lanes] = jnp.where(low, ot[:GRID_W], ot[GRID_W:]).astype(BF16)
        return carry

    lax.fori_loop(0, NA_Q_ROWS, row, 0, unroll=NA_UNROLL)


def _attn_kernel(*refs, nblk, rows):
    _win_kernel(*refs[:9], refs[13], refs[15], refs[16], nblk=nblk)
    _na_kernel(*refs[9:13], refs[14], rows=rows)


def _attn(bq, bk, bvt, sink, win_bias, cq, ck, cv, na_bias):
    bsz, seq, _ = bq.shape
    nblk = seq // BLOCK
    rows = seq // GRID_W
    assert nblk >= 2 and nblk % WIN_QB == 0 and WIN_QB * BLOCK == NA_Q_ROWS * GRID_W

    def edge(shift):
        return lambda b, i: jnp.clip(i * WIN_QB + shift, 0, nblk - 1)

    def k_edge(shift):
        return pl.BlockSpec((1, BLOCK, WIN_KV), lambda b, i: (b, edge(shift)(b, i), 0))

    def v_edge(shift):
        return pl.BlockSpec((1, WIN_KV, BLOCK), lambda b, i: (b, 0, edge(shift)(b, i)))

    span = WIN_QB * BLOCK
    qspec = pl.BlockSpec((1, span, WIN_Q), lambda b, i: (b, i, 0))
    cqspec = pl.BlockSpec((1, span, NA_W), lambda b, i: (b, i, 0))
    ckvspec = pl.BlockSpec((1, seq, NA_W), lambda b, i: (b, 0, 0))
    return pl.pallas_call(
        functools.partial(_attn_kernel, nblk=nblk, rows=rows),
        grid=(bsz, nblk // WIN_QB),
        in_specs=[pl.BlockSpec(memory_space=pltpu.SMEM), qspec,
                  k_edge(-1), pl.BlockSpec((1, span, WIN_KV), lambda b, i: (b, i, 0)), k_edge(WIN_QB),
                  v_edge(-1), pl.BlockSpec((1, WIN_KV, span), lambda b, i: (b, 0, i)), v_edge(WIN_QB),
                  _resident(win_bias.shape),
                  cqspec, ckvspec, ckvspec, _resident(na_bias.shape)],
        out_specs=[qspec, cqspec],
        out_shape=[jax.ShapeDtypeStruct((bsz, seq, WIN_Q), BF16), jax.ShapeDtypeStruct((bsz, seq, NA_W), BF16)],
        scratch_shapes=[pltpu.VMEM((WIN_QB + 2, BLOCK, WIN_KV), BF16),
                        pltpu.VMEM((WIN_QB + 2, WIN_KV, BLOCK), BF16)],
        compiler_params=_params("parallel", "arbitrary"),
        name="attn",
    )(sink, bq, bk, bk, bk, bvt, bvt, bvt, win_bias, cq, ck, cv, na_bias)


def _mixout_kernel(x_ref, g_ref, zprev_ref, z_ref, znext_ref, cw_ref, cb_ref, clg_ref, clb_ref,
                   ob_ref, oc_ref, wga_ref, wgb_ref, wgc_ref, wa_ref, wb_ref, wc_ref, wo_ref,
                   o_ref, y_ref, *, tiles_per_seq):
    x = x_ref[...]
    h = _rms(x, g_ref[...]).astype(BF16)
    chunks = [slice(c * MIX_CHUNK, (c + 1) * MIX_CHUNK) for c in range(D_MODEL // MIX_CHUNK)]

    def gated(wg_ref, b_ref, w_ref, sl):
        return jax.nn.sigmoid(_dot(h, wg_ref[:, sl])) * _dot(b_ref[...], w_ref[:, sl])

    pos = pl.program_id(0) % tiles_per_seq
    before = jnp.where(pos != 0, zprev_ref[...], 0.0)
    after = jnp.where(pos != tiles_per_seq - 1, znext_ref[...], 0.0)
    n_chunks = TOKEN_TILE // CONV_ROWS
    conv_args = (cw_ref[...], cb_ref[...], clg_ref[...], clb_ref[...])

    def conv(c):
        lo, hi = c * CONV_ROWS - CONV_PAD, (c + 1) * CONV_ROWS + CONV_PAD
        parts = ([before] if c == 0 else []) + [z_ref[max(lo, 0):min(hi, TOKEN_TILE), :]] \
            + ([after] if c == n_chunks - 1 else [])
        slab = parts[0] if len(parts) == 1 else jnp.concatenate(parts, axis=0)
        return _conv_chunk(slab, *conv_args)

    za = jnp.concatenate([conv(c) for c in range(n_chunks)], axis=0)
    for sl in chunks:
        y = gated(wgb_ref, ob_ref, wb_ref, sl) + gated(wgc_ref, oc_ref, wc_ref, sl)
        y = y + jax.nn.sigmoid(_dot(h, wga_ref[:, sl])) * _dot(za, wa_ref[:, sl])
        y_ref[:, sl] = y.astype(BF16)
    o_ref[...] = x + _dot(y_ref[...], wo_ref[...])


def _mixout(x, g, z, conv_p, ob, oc, w_in, wa, wb, wc, wo, l, seq):
    t, d = x.shape
    assert OFF_GATE % d == 0 and seq % TOKEN_TILE == 0 and TOKEN_TILE % CONV_ROWS == 0
    halo_per_tile = TOKEN_TILE // CONV_PAD
    n_halo = t // CONV_PAD

    def tile(n):
        return pl.BlockSpec((TOKEN_TILE, n), lambda i: (i, 0))

    def gate_cols(c):
        blk = OFF_GATE // d + c
        return pl.BlockSpec((None, d, d), lambda i: (l, 0, blk), pipeline_mode=pl.Buffered(1))

    z_before = pl.BlockSpec((CONV_PAD, CONV_CH), lambda i: (jnp.maximum(i * halo_per_tile - 1, 0), 0))
    z_after = pl.BlockSpec((CONV_PAD, CONV_CH), lambda i: (jnp.minimum((i + 1) * halo_per_tile, n_halo - 1), 0))
    return pl.pallas_call(
        functools.partial(_mixout_kernel, tiles_per_seq=seq // TOKEN_TILE),
        grid=(t // TOKEN_TILE,),
        in_specs=[tile(d), _resident((1, d)), z_before, tile(CONV_CH), z_after]
                 + [_resident(p.shape) for p in conv_p]
                 + [tile(WIN_Q), tile(NA_W), gate_cols(0), gate_cols(1), gate_cols(2),
                    _layer(wa, l), _layer(wb, l), _layer(wc, l), _layer(wo, l)],
        out_specs=tile(d),
        out_shape=jax.ShapeDtypeStruct((t, d), F32),
        scratch_shapes=[pltpu.VMEM((TOKEN_TILE, d), BF16)],
        compiler_params=_params("parallel"),
        name="mixout",
    )(x, g, z, z, z, *conv_p, ob, oc, w_in, w_in, w_in, wa, wb, wc, wo)


def _cross_kernel(x_ref, g_ref, m_ref, gm_ref, wkv_ref, wq_ref, wo_ref, o_ref, k_ref, v_ref):
    @pl.when(pl.program_id(1) == 0)
    def _():
        hm = _rms(m_ref[0], gm_ref[...]).astype(BF16)
        k_ref[...] = _dot(hm, wkv_ref[:, :D_MODEL]).astype(BF16)
        v_ref[...] = _dot(hm, wkv_ref[:, D_MODEL:]).astype(BF16)

    x = x_ref[0]
    h = _rms(x, g_ref[...]).astype(BF16)
    scale = X_HEAD_DIM ** -0.5 * LOG2E
    q = (_dot(h, wq_ref[...]) * scale).astype(BF16)
    k = k_ref[...]
    v = v_ref[...]
    outs = []
    for hd in range(X_HEADS):
        hs = slice(hd * X_HEAD_DIM, (hd + 1) * X_HEAD_DIM)
        s = _dot_nt(q[:, hs], k[:, hs])
        m = jnp.max(s, axis=-1, keepdims=True)
        p = jnp.exp2(s - m)
        den = jnp.sum(p, axis=-1, keepdims=True)
        outs.append((_dot(p.astype(BF16), v[:, hs]) * (1.0 / den)).astype(BF16))
    o = jnp.concatenate(outs, axis=-1)
    o_ref[0] = x + _dot(o, wo_ref[...])


def _cross(x, g, mem, gm, wkv, wq, wo, l):
    bsz, seq, d = x.shape
    mlen = mem.shape[1]
    tile = pl.BlockSpec((1, TOKEN_TILE, d), lambda b, i: (b, i, 0))
    mblk = pl.BlockSpec((1, mlen, d), lambda b, i: (b, 0, 0))
    return pl.pallas_call(
        _cross_kernel,
        grid=(bsz, seq // TOKEN_TILE),
        in_specs=[tile, _resident((1, d)), mblk, _resident((1, d)), _layer(wkv, l), _layer(wq, l), _layer(wo, l)],
        out_specs=tile,
        out_shape=jax.ShapeDtypeStruct((bsz, seq, d), F32),
        scratch_shapes=[pltpu.VMEM((mlen, d), BF16), pltpu.VMEM((mlen, d), BF16)],
        compiler_params=_params("parallel", "arbitrary"),
        name="cross",
    )(x, g, mem, gm, wkv, wq, wo)


def kernel(x, mem, norm_ffn1, ffn1_w_gate, ffn1_w_up, ffn1_w_down, norm_mix, w_in, conv_dw_w, conv_dw_b, conv_ln_g, conv_ln_b, conv_w_out, win_sink, t5_bias, win_w_out, na_rpb, na_w_out, w_out, norm_cross, norm_mem, cross_w_q, cross_w_kv, cross_w_o, norm_ffn2, ffn2_w_gate, ffn2_w_up, ffn2_w_down, norm_final):
    bsz, seq, d = x.shape
    depth = w_in.shape[0]
    t = bsz * seq
    assert d == D_MODEL and t % TOKEN_TILE == 0 and seq % TOKEN_TILE == 0
    assert seq % (NA_Q_ROWS * GRID_W) == 0 and seq // GRID_W >= NA_ROWS and seq % CONV_ROWS == 0

    def row(v):
        return v.reshape(1, -1).astype(F32)

    def bf(w):
        return w.astype(BF16)

    win_bias = _win_bias(t5_bias * LOG2E)
    win_heads = [h for tl in range(WIN_PAIRS) for h in (tl, tl + WIN_GROUP)]
    gf = row(norm_final)
    ffn1 = (bf(ffn1_w_gate), bf(ffn1_w_up), bf(ffn1_w_down))
    late = (w_in, conv_w_out, win_w_out, na_w_out, w_out, cross_w_q, cross_w_kv, cross_w_o)
    late_ffn = (ffn2_w_gate, ffn2_w_up, ffn2_w_down)

    def flat(w):
        return w.reshape(-1, w.shape[-1]).astype(F32)

    xt = x.reshape(t, d)
    for l in range(depth):
        xt, cast = _ffn(xt, row(norm_ffn1[l]), *ffn1, gf, False, l,
                        side=[flat(w) for w in late] if l == 0 else ())
        if l == 0:
            (w_in_b, conv_w_out_b, win_w_out_b, na_w_out_b, w_out_b, cross_w_q_b, cross_w_kv_b,
             cross_w_o_b) = [c.reshape(w.shape) for c, w in zip(cast, late)]
            w_bq = jnp.concatenate(
                [w_in_b[:, :, OFF_BQ + h * HEAD_DIM:OFF_BQ + (h + 1) * HEAD_DIM] for h in win_heads], axis=2)
            win_w_out_b = jnp.concatenate(
                [win_w_out_b[:, h * HEAD_DIM:(h + 1) * HEAD_DIM] for h in win_heads], axis=1)
            mix_w = (conv_w_out_b, win_w_out_b, na_w_out_b, w_out_b)
            cross_w = (cross_w_q_b, cross_w_o_b)
        (z, bq, bk, bv, cq, ck, cv), cast = _inproj(xt, row(norm_mix[l]), w_in_b, w_bq, l,
                                                    side=[flat(w) for w in late_ffn] if l == 0 else ())
        if l == 0:
            ffn2 = [c.reshape(w.shape) for c, w in zip(cast, late_ffn)]

        def seqs(a):
            return a.reshape(bsz, seq, a.shape[-1])

        conv_p = (conv_dw_w[l].astype(F32), row(conv_dw_b[l]), row(conv_ln_g[l]), row(conv_ln_b[l]))
        ob, oc = _attn(seqs(bq), seqs(bk), jnp.swapaxes(seqs(bv), 1, 2), win_sink[l].astype(F32) * LOG2E, win_bias,
                       seqs(cq), seqs(ck), seqs(cv), _na_bias(na_rpb[l] * LOG2E))
        xt = _mixout(xt, row(norm_mix[l]), z, conv_p, ob.reshape(t, -1), oc.reshape(t, -1), w_in_b,
                     *mix_w, l, seq)
        xt = _cross(xt.reshape(bsz, seq, d), row(norm_cross[l]), mem.astype(F32), row(norm_mem[l]),
                    cross_w_kv_b, *cross_w, l).reshape(t, d)
        xt, _ = _ffn(xt, row(norm_ffn2[l]), *ffn2, gf, l == depth - 1, l)
    return xt.reshape(bsz, seq, d)
```
